```python
import jax, jax.numpy as jnp
from jax import lax
import numpy as np

D_MODEL = 2048
BATCH = 16
SEQ = 256
DEPTH = 1
DEC_BATCH = 8
DEC_SEQ = 1024
PAST_LEN = 512

GRID_W = 64
N_HEADS = 16
Q_RANK = 512
KV_RANK = 512
QK_NOPE = 128
QK_ROPE = 64
V_DIM = 128
ROPE_THETA = 10000.0
Q_BLOCK = 128
D_INNER = 2 * D_MODEL
SSD_HEADDIM = 64
SSD_HEADS = D_INNER // SSD_HEADDIM
SSD_GROUPS = 8
D_STATE = 128
CONV_W = 5
CHUNK = 128
XBC_DIM = D_INNER + 2 * SSD_GROUPS * D_STATE
N_EXPERTS = 16
CAP_FACTOR = 2
D_EXPERT = 1536
EPS = 1e-6
IN_DIM = Q_RANK + KV_RANK + QK_ROPE + D_INNER + XBC_DIM + 2 * SSD_HEADS + 2 * D_MODEL

kernel_name = "hybrid_mla_ssd_ec_diffusion_step"


def rmsnorm(x, g):
    xf = x.astype(jnp.float32)
    y = xf * lax.rsqrt(jnp.mean(xf * xf, axis=-1, keepdims=True) + EPS)
    return (y * g.astype(jnp.float32)).astype(x.dtype)


def adaln(cvec, w_mod, b_mod):
    m = jax.nn.silu(cvec) @ w_mod + b_mod
    return jnp.split(m[:, None, :], 6, axis=-1)


def axial_rope_tables(n_tok):
    rows = n_tok // GRID_W
    row = jnp.repeat(jnp.arange(rows), GRID_W).astype(jnp.float32)
    col = jnp.tile(jnp.arange(GRID_W), rows).astype(jnp.float32)
    half = QK_ROPE // 2
    inv = ROPE_THETA ** (-jnp.arange(0, half, 2, dtype=jnp.float32) / half)
    ang_r = row[:, None] * inv[None, :]
    ang_c = col[:, None] * inv[None, :]
    ang = jnp.concatenate([ang_r, ang_r, ang_c, ang_c], axis=-1)
    return jnp.cos(ang), jnp.sin(ang)


def rotate_half_axial(x):
    x1, x2, x3, x4 = jnp.split(x, 4, axis=-1)
    return jnp.concatenate([-x2, x1, -x4, x3], axis=-1)


def apply_rope(x, cos, sin):
    xf = x.astype(jnp.float32)
    return (xf * cos + rotate_half_axial(xf) * sin).astype(x.dtype)


def mla_attend(q_nope, q_rope, k_nope, k_rope, v):
    b, lq, h, _ = q_nope.shape
    nb = lq // Q_BLOCK
    scale = (QK_NOPE + QK_ROPE) ** -0.5
    qn = q_nope.reshape(b, nb, Q_BLOCK, h, QK_NOPE).transpose(1, 0, 2, 3, 4)
    qr = q_rope.reshape(b, nb, Q_BLOCK, h, QK_ROPE).transpose(1, 0, 2, 3, 4)

    def block(args):
        qn_b, qr_b = args
        s = (jnp.einsum('bqhd,bkhd->bhqk', qn_b, k_nope)
             + jnp.einsum('bqhr,bkr->bhqk', qr_b, k_rope))
        p = jax.nn.softmax(s.astype(jnp.float32) * scale, axis=-1).astype(v.dtype)
        return jnp.einsum('bhqk,bkhv->bqhv', p, v)

    o = lax.map(block, (qn, qr))
    return o.transpose(1, 0, 2, 3, 4).reshape(b, lq, h * V_DIM)


def dwconv_centred(u, w, bias):
    pad = CONV_W // 2
    y = lax.conv_general_dilated(u, w[:, None, :].astype(u.dtype), window_strides=(1,),
                                 padding=[(pad, pad)],
                                 dimension_numbers=('NWC', 'WIO', 'NWC'),
                                 feature_group_count=u.shape[-1])
    return y + bias


def segsum(a):
    cs = jnp.cumsum(a, axis=-1)
    diff = cs[..., :, None] - cs[..., None, :]
    n = a.shape[-1]
    mask = jnp.tril(jnp.ones((n, n), dtype=bool))
    return jnp.where(mask, diff, -jnp.inf)


def ssd_scan(x, dt, A, Bm, Cm, init):
    b, L, h, p = x.shape
    g, n = Bm.shape[2], Bm.shape[3]
    r = h // g
    c = L // CHUNK
    f32 = jnp.float32
    xc = x.astype(f32).reshape(b, c, CHUNK, g, r, p)
    dtc = dt.astype(f32).reshape(b, c, CHUNK, g, r)
    Bc = Bm.astype(f32).reshape(b, c, CHUNK, g, n)
    Cc = Cm.astype(f32).reshape(b, c, CHUNK, g, n)
    xdt = xc * dtc[..., None]
    dA = (dtc * A.astype(f32).reshape(g, r)).transpose(0, 3, 4, 1, 2)
    cs = jnp.cumsum(dA, axis=-1)
    decay_in = jnp.exp(segsum(dA))
    cb = jnp.einsum('bclgn,bcsgn->bcgls', Cc, Bc)
    y_diag = jnp.einsum('bcgls,bgrcls,bcsgrp->bclgrp', cb, decay_in, xdt)
    decay_to_end = jnp.exp(cs[..., -1:] - cs)
    chunk_states = jnp.einsum('bclgn,bgrcl,bclgrp->bcgrpn', Bc, decay_to_end, xdt)
    states_all = jnp.concatenate(
        [init.astype(f32).reshape(b, 1, g, r, p, n), chunk_states], axis=1)
    chunk_decay = jnp.exp(segsum(jnp.pad(cs[..., -1], ((0, 0), (0, 0), (0, 0), (1, 0)))))
    states_new = jnp.einsum('bgrzc,bcgrpn->bzgrpn', chunk_decay, states_all)
    states_in, final = states_new[:, :-1], states_new[:, -1]
    y_off = jnp.einsum('bclgn,bcgrpn,bgrcl->bclgrp', Cc, states_in, jnp.exp(cs))
    y = (y_diag + y_off).reshape(b, L, h, p).astype(x.dtype)
    return y, final.reshape(b, h, p, n).astype(x.dtype)


def token_mixer(h, lp, ssm_init, ctx_ckv, ctx_krope):
    latent = ctx_ckv is not None
    b, L, _ = h.shape
    proj = h @ lp['w_in']
    sizes = (Q_RANK, KV_RANK, QK_ROPE, D_INNER, XBC_DIM, 2 * SSD_HEADS)
    cuts = tuple(int(v) for v in np.cumsum(sizes))
    cq, ckv_raw, krope, z, xbc, dt_raw, gate_raw = jnp.split(proj, cuts, axis=-1)

    q = (rmsnorm(cq, lp['q_norm_g']) @ lp['w_q_b']).reshape(b, L, N_HEADS, QK_NOPE + QK_ROPE)
    q_nope, q_rope = q[..., :QK_NOPE], q[..., QK_NOPE:]
    ckv = rmsnorm(ckv_raw, lp['kv_norm_g'])
    if latent:
        cos, sin = axial_rope_tables(L)
        q_rope = apply_rope(q_rope, cos[:, None, :], sin[:, None, :])
        keys_ckv = jnp.concatenate([ctx_ckv.astype(ckv.dtype), ckv], axis=1)
        keys_krope = jnp.concatenate([ctx_krope.astype(krope.dtype), apply_rope(krope, cos, sin)], axis=1)
    else:
        keys_ckv, keys_krope = ckv, krope
    lk = keys_ckv.shape[1]
    kv = (keys_ckv @ lp['w_kv_b']).reshape(b, lk, N_HEADS, QK_NOPE + V_DIM)
    k_nope, v = kv[..., :QK_NOPE], kv[..., QK_NOPE:]
    o_mla = mla_attend(q_nope, q_rope, k_nope, keys_krope, v) @ lp['w_o_mla']

    xbc = jax.nn.silu(dwconv_centred(xbc, lp['conv_w'], lp['conv_b']))
    xs, Bm, Cm = jnp.split(xbc, (D_INNER, D_INNER + SSD_GROUPS * D_STATE), axis=-1)
    xs = xs.reshape(b, L, SSD_HEADS, SSD_HEADDIM)
    Bm = Bm.reshape(b, L, SSD_GROUPS, D_STATE)
    Cm = Cm.reshape(b, L, SSD_GROUPS, D_STATE)
    dt = jax.nn.softplus(dt_raw.reshape(b, L, 2, SSD_HEADS).astype(jnp.float32)
                         + lp['dt_bias'].astype(jnp.float32))
    A = -jnp.exp(lp['a_log'].astype(jnp.float32))
    y_f, s_f = ssd_scan(xs, dt[:, :, 0], A[0], Bm, Cm, ssm_init[:, 0])
    y_b, s_b = ssd_scan(xs[:, ::-1], dt[:, ::-1, 1], A[1], Bm[:, ::-1], Cm[:, ::-1], ssm_init[:, 1])
    y = y_f + y_b[:, ::-1] + lp['d_skip'][:, None] * xs
    y = y.reshape(b, L, D_INNER)
    o_ssd = rmsnorm(y * jax.nn.silu(z), lp['ssd_norm_g']) @ lp['w_o_ssd']
    ssm_final = jnp.stack([s_f, s_b], axis=1)

    g_mla, g_ssd = jnp.split(jax.nn.sigmoid(gate_raw), 2, axis=-1)
    out = (g_mla * o_mla + g_ssd * o_ssd) @ lp['w_out']
    return out, ckv, krope, ssm_final


def ec_moe(h, w_router, w_e_gate, w_e_up, w_e_down):
    b, L, d = h.shape
    t = h.reshape(b * L, d)
    cap = CAP_FACTOR * (b * L) // N_EXPERTS
    aff = jax.nn.softmax((t @ w_router).astype(jnp.float32), axis=-1)
    gval, idx = lax.top_k(aff.T, cap)
    xe = t[idx]
    hid = jax.nn.silu(jnp.einsum('ecd,edf->ecf', xe, w_e_gate)) * jnp.einsum('ecd,edf->ecf', xe, w_e_up)
    ye = jnp.einsum('ecf,efd->ecd', hid, w_e_down) * gval[..., None].astype(h.dtype)
    out = jnp.zeros_like(t).at[idx.reshape(-1)].add(ye.reshape(-1, d))
    return out.reshape(b, L, d)


def trunk_layer(x, mod, lp, ssm_init, ctx_ckv, ctx_krope):
    sh1, sc1, g1, sh2, sc2, g2 = mod
    h = rmsnorm(x, lp['pre1_g']) * (1.0 + sc1) + sh1
    out, ckv, krope, ssm_final = token_mixer(h, lp, ssm_init, ctx_ckv, ctx_krope)
    x = x + g1 * rmsnorm(out, lp['post1_g'])
    h = rmsnorm(x, lp['pre2_g']) * (1.0 + sc2) + sh2
    moe = ec_moe(h, lp['w_router'], lp['w_e_gate'], lp['w_e_up'], lp['w_e_down'])
    x = x + g2 * rmsnorm(moe, lp['post2_g'])
    return x, ckv, krope, ssm_final


def setup_inputs(seed: int = 0) -> dict:
    key = jax.random.key(seed)
    ks = jax.random.split(key, 40)
    f32 = jnp.float32

    def nrm(k, shape, scale):
        return jax.random.normal(k, shape, f32) * scale

    def gain(k, shape):
        return 1.0 + 0.05 * jax.random.normal(k, shape, f32)

    dt_init = jnp.exp(jax.random.uniform(ks[30], (DEPTH, 2, SSD_HEADS), f32,
                                         np.log(1e-3), np.log(1e-1)))
    return {
        "x_prompt": nrm(ks[0], (BATCH, SEQ, D_MODEL), 1.0),
        "x_sample": nrm(ks[1], (DEC_BATCH, DEC_SEQ, D_MODEL), 1.0),
        "cache_ckv": nrm(ks[2], (DEC_BATCH, DEPTH, PAST_LEN, KV_RANK), 1.0),
        "cache_krope": nrm(ks[3], (DEC_BATCH, DEPTH, PAST_LEN, QK_ROPE), 1.0),
        "state_ssm": nrm(ks[4], (DEC_BATCH, DEPTH, 2, SSD_HEADS, SSD_HEADDIM, D_STATE), 0.5),
        "c": nrm(ks[5], (DEC_BATCH, D_MODEL), 1.0),
        "c_ctx": nrm(ks[6], (D_MODEL,), 1.0),
        "w_mod": nrm(ks[7], (DEPTH, D_MODEL, 6 * D_MODEL), 0.5 * D_MODEL ** -0.5),
        "b_mod": nrm(ks[8], (DEPTH, 6 * D_MODEL), 0.02),
        "pre1_g": gain(ks[9], (DEPTH, D_MODEL)),
        "post1_g": gain(ks[10], (DEPTH, D_MODEL)),
        "pre2_g": gain(ks[11], (DEPTH, D_MODEL)),
        "post2_g": gain(ks[12], (DEPTH, D_MODEL)),
        "w_in": nrm(ks[13], (DEPTH, D_MODEL, IN_DIM), D_MODEL ** -0.5),
        "q_norm_g": gain(ks[14], (DEPTH, Q_RANK)),
        "w_q_b": nrm(ks[15], (DEPTH, Q_RANK, N_HEADS * (QK_NOPE + QK_ROPE)), Q_RANK ** -0.5),
        "kv_norm_g": gain(ks[16], (DEPTH, KV_RANK)),
        "w_kv_b": nrm(ks[17], (DEPTH, KV_RANK, N_HEADS * (QK_NOPE + V_DIM)), KV_RANK ** -0.5),
        "w_o_mla": nrm(ks[18], (DEPTH, N_HEADS * V_DIM, D_MODEL), (N_HEADS * V_DIM) ** -0.5),
        "conv_w": nrm(ks[19], (DEPTH, CONV_W, XBC_DIM), CONV_W ** -0.5),
        "conv_b": nrm(ks[20], (DEPTH, XBC_DIM), 0.02),
        "dt_bias": dt_init + jnp.log(-jnp.expm1(-dt_init)),
        "a_log": jnp.log(jax.random.uniform(ks[21], (DEPTH, 2, SSD_HEADS), f32, 1.0, 16.0)),
        "d_skip": gain(ks[22], (DEPTH, SSD_HEADS)),
        "ssd_norm_g": gain(ks[23], (DEPTH, D_INNER)),
        "w_o_ssd": nrm(ks[24], (DEPTH, D_INNER, D_MODEL), D_INNER ** -0.5),
        "w_out": nrm(ks[25], (DEPTH, D_MODEL, D_MODEL), D_MODEL ** -0.5),
        "w_router": nrm(ks[26], (DEPTH, D_MODEL, N_EXPERTS), D_MODEL ** -0.5),
        "w_e_gate": nrm(ks[27], (DEPTH, N_EXPERTS, D_MODEL, D_EXPERT), D_MODEL ** -0.5),
        "w_e_up": nrm(ks[28], (DEPTH, N_EXPERTS, D_MODEL, D_EXPERT), D_MODEL ** -0.5),
        "w_e_down": nrm(ks[29], (DEPTH, N_EXPERTS, D_EXPERT, D_MODEL), D_EXPERT ** -0.5),
    }


def reference(x_prompt, x_sample, cache_ckv, cache_krope, state_ssm, c, c_ctx,
              w_mod, b_mod, pre1_g, post1_g, pre2_g, post2_g, w_in, q_norm_g, w_q_b,
              kv_norm_g, w_kv_b, w_o_mla, conv_w, conv_b, dt_bias, a_log, d_skip,
              ssd_norm_g, w_o_ssd, w_out, w_router, w_e_gate, w_e_up, w_e_down):
    y_p = x_prompt
    y_s = x_sample
    ckv_list, krope_list, ssm_list = [], [], []
    for l in range(DEPTH):
        lp = {
            'pre1_g': pre1_g[l], 'post1_g': post1_g[l], 'pre2_g': pre2_g[l], 'post2_g': post2_g[l],
            'w_in': w_in[l], 'q_norm_g': q_norm_g[l], 'w_q_b': w_q_b[l],
            'kv_norm_g': kv_norm_g[l], 'w_kv_b': w_kv_b[l], 'w_o_mla': w_o_mla[l],
            'conv_w': conv_w[l], 'conv_b': conv_b[l], 'dt_bias': dt_bias[l], 'a_log': a_log[l],
            'd_skip': d_skip[l], 'ssd_norm_g': ssd_norm_g[l], 'w_o_ssd': w_o_ssd[l],
            'w_out': w_out[l], 'w_router': w_router[l], 'w_e_gate': w_e_gate[l],
            'w_e_up': w_e_up[l], 'w_e_down': w_e_down[l],
        }
        mod_ctx = adaln(c_ctx[None, :], w_mod[l], b_mod[l])
        mod_lat = adaln(c, w_mod[l], b_mod[l])
        zero_state = jnp.zeros((y_p.shape[0], 2, SSD_HEADS, SSD_HEADDIM, D_STATE), y_p.dtype)
        y_p, ckv_new, krope_new, ssm_new = trunk_layer(y_p, mod_ctx, lp, zero_state, None, None)
        ckv_list.append(ckv_new)
        krope_list.append(krope_new)
        ssm_list.append(ssm_new)
        y_s, _, _, _ = trunk_layer(y_s, mod_lat, lp, state_ssm[:, l], cache_ckv[:, l], cache_krope[:, l])
    new_ckv = jnp.stack(ckv_list, axis=1)
    new_krope = jnp.stack(krope_list, axis=1)
    new_ssm = jnp.stack(ssm_list, axis=1)
    return (y_p, y_s, new_ckv, new_krope, new_ssm)
```

```python
import functools

import jax
import jax.numpy as jnp
import numpy as np
from jax import lax
from jax.experimental import pallas as pl
from jax.experimental.pallas import tpu as pltpu

F32 = jnp.float32
BF16 = jnp.bfloat16

D_MODEL = 2048
BATCH = 16
SEQ = 256
DEC_BATCH = 8
DEC_SEQ = 1024
PAST_LEN = 512
GRID_W = 64
N_HEADS = 16
Q_RANK = 512
KV_RANK = 512
QK_NOPE = 128
QK_ROPE = 64
V_DIM = 128
ROPE_THETA = 10000.0
D_INNER = 2 * D_MODEL
SSD_HEADDIM = 64
SSD_HEADS = D_INNER // SSD_HEADDIM
SSD_GROUPS = 8
HEADS_PER_GROUP = SSD_HEADS // SSD_GROUPS
D_STATE = 128
CONV_W = 5
CHUNK = 128
XBC_DIM = D_INNER + 2 * SSD_GROUPS * D_STATE
N_EXPERTS = 16
CAP_FACTOR = 2
D_EXPERT = 1536
EPS = 1e-6

N_CTX = BATCH * SEQ
N_LAT = DEC_BATCH * DEC_SEQ
N_TOK = N_CTX + N_LAT
Q_HEAD_PAD = 256
GROUP_W = HEADS_PER_GROUP * SSD_HEADDIM
LANES = 128
VMEM_LIMIT = 56 * 1024 * 1024
FFN_ROWS = 512


def _params(*sem):
    return pltpu.CompilerParams(dimension_semantics=sem, vmem_limit_bytes=VMEM_LIMIT)


def _rms(x):
    return x * lax.rsqrt(jnp.mean(x * x, axis=-1, keepdims=True) + EPS)


def _silu(x):
    return x * jax.nn.sigmoid(x)


def _mod_row(tm):
    n_ctx_tiles = N_CTX // tm
    per_lat = DEC_SEQ // tm
    return lambda i: jnp.where(i < n_ctx_tiles, 0, 1 + (i - n_ctx_tiles) // per_lat)


def _rope_row(tm):
    n_ctx_tiles = N_CTX // tm
    per_lat = DEC_SEQ // tm
    return lambda i: jnp.where(i < n_ctx_tiles, 0, 1 + (i - n_ctx_tiles) % per_lat)


def _rot_half(x):
    w = x.shape[1]
    nxt = pltpu.roll(x, w - 16, axis=1)
    prv = pltpu.roll(x, 16, axis=1)
    lane = lax.broadcasted_iota(jnp.int32, x.shape, 1)
    return jnp.where((lane % 32) < 16, -nxt, prv)


def _adaln_kernel(c_ref, w_ref, b_ref, o_ref):
    a = _silu(c_ref[...]).astype(BF16)
    o_ref[...] = jnp.dot(a, w_ref[...].astype(BF16), preferred_element_type=F32) + b_ref[...]


def _adaln(cvec, w_mod, b_mod):
    rows = cvec.shape[0]
    n = w_mod.shape[1]
    tn = 1024
    return pl.pallas_call(
        _adaln_kernel,
        grid=(n // tn,),
        in_specs=[pl.BlockSpec((rows, D_MODEL), lambda j: (0, 0)),
                  pl.BlockSpec((D_MODEL, tn), lambda j: (0, j)),
                  pl.BlockSpec((1, tn), lambda j: (0, j))],
        out_specs=pl.BlockSpec((rows, tn), lambda j: (0, j)),
        out_shape=jax.ShapeDtypeStruct((rows, n), F32),
        compiler_params=_params("arbitrary"),
        name="adaln",
    )(cvec, w_mod, b_mod.reshape(1, n))


def _prenorm_kernel(x_ref, g_ref, sc_ref, sh_ref, o_ref):
    y = _rms(x_ref[...]) * g_ref[...]
    o_ref[...] = (y * (1.0 + sc_ref[0]) + sh_ref[0]).astype(o_ref.dtype)


def _prenorm(x, gain, mod, sc_blk, sh_blk):
    tm = 256
    r = _mod_row(tm)
    return pl.pallas_call(
        _prenorm_kernel,
        grid=(N_TOK // tm,),
        in_specs=[pl.BlockSpec((tm, D_MODEL), lambda i: (i, 0)),
                  pl.BlockSpec((1, D_MODEL), lambda i: (0, 0)),
                  pl.BlockSpec((1, 1, D_MODEL), lambda i: (r(i), 0, sc_blk)),
                  pl.BlockSpec((1, 1, D_MODEL), lambda i: (r(i), 0, sh_blk))],
        out_specs=pl.BlockSpec((tm, D_MODEL), lambda i: (i, 0)),
        out_shape=jax.ShapeDtypeStruct((N_TOK, D_MODEL), BF16),
        compiler_params=_params("arbitrary"),
        name="prenorm",
    )(x, gain.reshape(1, D_MODEL), mod, mod)


def _inproj_small_kernel(h_ref, w_ref, qg_ref, kvg_ref, cos_ref, sin_ref,
                         cqn_ref, ckv_ref, kr_ref, dt_ref):
    p = jnp.dot(h_ref[...], w_ref[...], preferred_element_type=F32)
    cqn_ref[...] = (_rms(p[:, :Q_RANK]) * qg_ref[...]).astype(cqn_ref.dtype)
    ckv_ref[...] = _rms(p[:, Q_RANK:Q_RANK + KV_RANK]) * kvg_ref[...]
    kr = p[:, Q_RANK + KV_RANK:Q_RANK + KV_RANK + LANES]
    kr_ref[...] = kr * cos_ref[...] + _rot_half(kr) * sin_ref[...]
    dt_ref[...] = p[:, Q_RANK + KV_RANK + LANES:]


def _inproj_small(h, w, q_norm_g, kv_norm_g, cos_k, sin_k):
    tm = 256
    n = w.shape[1]
    rr = _rope_row(tm)
    row = lambda i: (i, 0)
    return pl.pallas_call(
        _inproj_small_kernel,
        grid=(N_TOK // tm,),
        in_specs=[pl.BlockSpec((tm, D_MODEL), row),
                  pl.BlockSpec((D_MODEL, n), lambda i: (0, 0)),
                  pl.BlockSpec((1, Q_RANK), lambda i: (0, 0)),
                  pl.BlockSpec((1, KV_RANK), lambda i: (0, 0)),
                  pl.BlockSpec((tm, LANES), lambda i: (rr(i), 0)),
                  pl.BlockSpec((tm, LANES), lambda i: (rr(i), 0))],
        out_specs=[pl.BlockSpec((tm, Q_RANK), row),
                   pl.BlockSpec((tm, KV_RANK), row),
                   pl.BlockSpec((tm, LANES), row),
                   pl.BlockSpec((tm, LANES), row)],
        out_shape=[jax.ShapeDtypeStruct((N_TOK, Q_RANK), BF16),
                   jax.ShapeDtypeStruct((N_TOK, KV_RANK), F32),
                   jax.ShapeDtypeStruct((N_TOK, LANES), F32),
                   jax.ShapeDtypeStruct((N_TOK, LANES), F32)],
        compiler_params=_params("arbitrary"),
        name="inproj_small",
    )(h, w, q_norm_g.reshape(1, Q_RANK), kv_norm_g.reshape(1, KV_RANK), cos_k, sin_k)


def _mm_kernel(a_ref, w_ref, o_ref, *, act):
    p = jnp.dot(a_ref[...], w_ref[...], preferred_element_type=F32)
    if act == "sigmoid":
        p = jax.nn.sigmoid(p)
    o_ref[...] = p.astype(o_ref.dtype)


def _mm(a, w, *, tm, tn, out_dtype, act=None, name="mm"):
    m, k = a.shape
    n = w.shape[1]
    return pl.pallas_call(
        functools.partial(_mm_kernel, act=act),
        grid=(n // tn, m // tm),
        in_specs=[pl.BlockSpec((tm, k), lambda j, i: (i, 0)),
                  pl.BlockSpec((k, tn), lambda j, i: (0, j))],
        out_specs=pl.BlockSpec((tm, tn), lambda j, i: (i, j)),
        out_shape=jax.ShapeDtypeStruct((m, n), out_dtype),
        compiler_params=_params("arbitrary", "arbitrary"),
        name=name,
    )(a, w)


def _qproj_kernel(a_ref, w_ref, cos_ref, sin_ref, o_ref):
    p = jnp.dot(a_ref[...], w_ref[...], preferred_element_type=F32)
    cos = cos_ref[...]
    sin = sin_ref[...]
    for h in range(N_HEADS):
        sl = slice(h * Q_HEAD_PAD, (h + 1) * Q_HEAD_PAD)
        ph = p[:, sl]
        o_ref[:, sl] = (ph * cos + _rot_half(ph) * sin).astype(o_ref.dtype)


def _qproj(cqn, wq, cos_q, sin_q):
    tm = 256
    n = wq.shape[1]
    rr = _rope_row(tm)
    return pl.pallas_call(
        _qproj_kernel,
        grid=(N_TOK // tm,),
        in_specs=[pl.BlockSpec((tm, Q_RANK), lambda i: (i, 0)),
                  pl.BlockSpec((Q_RANK, n), lambda i: (0, 0)),
                  pl.BlockSpec((tm, Q_HEAD_PAD), lambda i: (rr(i), 0)),
                  pl.BlockSpec((tm, Q_HEAD_PAD), lambda i: (rr(i), 0))],
        out_specs=pl.BlockSpec((tm, n), lambda i: (i, 0)),
        out_shape=jax.ShapeDtypeStruct((N_TOK, n), BF16),
        compiler_params=_params("arbitrary"),
        name="qproj",
    )(cqn, wq, cos_q, sin_q)


def _attn_kernel(q_ref, kn_ref, kr_ref, v_ref, o_ref, *, scale):
    k = jnp.concatenate([kn_ref[...], kr_ref[...]], axis=1)
    s = lax.dot_general(q_ref[...], k, (((1,), (1,)), ((), ())),
                        preferred_element_type=F32) * scale
    m = jnp.max(s, axis=-1, keepdims=True)
    p = jnp.exp(s - m)
    l = jnp.sum(p, axis=-1, keepdims=True)
    o = jnp.dot(p.astype(BF16), v_ref[...], preferred_element_type=F32)
    o_ref[...] = (o / l).astype(o_ref.dtype)


def _attention(q, kv, kr, *, n_batch, lq, lk, tq, q_row0):
    nq = lq // tq
    qb0 = q_row0 // tq
    scale = float((QK_NOPE + QK_ROPE) ** -0.5)
    return pl.pallas_call(
        functools.partial(_attn_kernel, scale=scale),
        grid=(n_batch, N_HEADS, nq),
        in_specs=[pl.BlockSpec((tq, Q_HEAD_PAD), lambda b, h, i: (qb0 + b * nq + i, h)),
                  pl.BlockSpec((lk, QK_NOPE), lambda b, h, i: (b, 2 * h)),
                  pl.BlockSpec((lk, LANES), lambda b, h, i: (b, 0)),
                  pl.BlockSpec((lk, V_DIM), lambda b, h, i: (b, 2 * h + 1))],
        out_specs=pl.BlockSpec((tq, V_DIM), lambda b, h, i: (b * nq + i, h)),
        out_shape=jax.ShapeDtypeStruct((n_batch * lq, N_HEADS * V_DIM), BF16),
        compiler_params=_params("arbitrary", "arbitrary", "arbitrary"),
        name="attention",
    )(q, kv, kr, kv)


def _omla_kernel(a_ref, w_ref, g_ref, o_ref):
    p = jnp.dot(a_ref[...], w_ref[...], preferred_element_type=F32)
    o_ref[...] = p * g_ref[...].astype(F32)


def _omla(attn_o, w, gates):
    tm, tn = 512, 1024
    return pl.pallas_call(
        _omla_kernel,
        grid=(D_MODEL // tn, N_TOK // tm),
        in_specs=[pl.BlockSpec((tm, D_MODEL), lambda j, i: (i, 0)),
                  pl.BlockSpec((D_MODEL, tn), lambda j, i: (0, j)),
                  pl.BlockSpec((tm, tn), lambda j, i: (i, j))],
        out_specs=pl.BlockSpec((tm, tn), lambda j, i: (i, j)),
        out_shape=jax.ShapeDtypeStruct((N_TOK, D_MODEL), F32),
        compiler_params=_params("arbitrary", "arbitrary"),
        name="omla",
    )(attn_o, w, gates)


def _split3(x):
    hi = x.astype(BF16)
    r1 = x - hi.astype(F32)
    mid = r1.astype(BF16)
    lo = (r1 - mid.astype(F32)).astype(BF16)
    return hi, mid, lo


def _dot3_lhs(x, sel):
    return sum(jnp.dot(p, sel, preferred_element_type=F32) for p in _split3(x))


def _dot3_rhs(sel, x):
    return sum(jnp.dot(sel, p, preferred_element_type=F32) for p in _split3(x))


def _softplus(x):
    return jnp.maximum(x, 0.0) + jnp.log1p(jnp.exp(-jnp.abs(x)))


def _conv_silu(u, w_ref, b_ref):
    n_rows = u.shape[0]
    t = lax.broadcasted_iota(jnp.int32, u.shape, 0)
    acc = b_ref[...] + w_ref[CONV_W // 2:CONV_W // 2 + 1, :] * u
    for k in range(CONV_W):
        off = k - CONV_W // 2
        if off == 0:
            continue
        shifted = pltpu.roll(u, (-off) % n_rows, axis=0)
        valid = (t + off >= 0) & (t + off < n_rows)
        acc = acc + w_ref[k:k + 1, :] * jnp.where(valid, shifted, 0.0)
    return _silu(acc)


def _ssd_kernel(*refs, seq_len, has_init):
    if has_init:
        (xs_ref, b_ref, c_ref, wx_ref, wb_ref, wc_ref, bx_ref, bb_ref, bc_ref,
         dt_ref, dtt_ref, bias_r_ref, bias_c_ref, a_r_ref, a_c_ref, dsk_ref, init_ref,
         y_ref, fin_ref, xs_s, bm_s, cm_s, dt_s, dtt_s, y_s) = refs
    else:
        (xs_ref, b_ref, c_ref, wx_ref, wb_ref, wc_ref, bx_ref, bb_ref, bc_ref,
         dt_ref, dtt_ref, bias_r_ref, bias_c_ref, a_r_ref, a_c_ref, dsk_ref,
         y_ref, fin_ref, xs_s, bm_s, cm_s, dt_s, dtt_s, y_s) = refs
        init_ref = None
    n_chunks = seq_len // CHUNK
    n_dh = 2 * HEADS_PER_GROUP

    xs_s[...] = _conv_silu(xs_ref[...].astype(F32), wx_ref, bx_ref)
    bm_s[...] = _conv_silu(b_ref[...].astype(F32), wb_ref, bb_ref)
    cm_s[...] = _conv_silu(c_ref[...].astype(F32), wc_ref, bc_ref)
    dt_s[...] = _softplus(dt_ref[0] + bias_r_ref[0])
    dtt_s[...] = _softplus(dtt_ref[0] + bias_c_ref[0])

    li = lax.broadcasted_iota(jnp.int32, (CHUNK, CHUNK), 0)
    si = lax.broadcasted_iota(jnp.int32, (CHUNK, CHUNK), 1)
    lane = lax.broadcasted_iota(jnp.int32, (CHUNK, LANES), 1)
    e_row = lax.broadcasted_iota(jnp.int32, (n_dh, GROUP_W), 0)
    e_col = lax.broadcasted_iota(jnp.int32, (n_dh, GROUP_W), 1)
    a_row = a_r_ref[0]
    a_col = a_c_ref[0]

    for d in range(2):
        keep = (li >= si) if d == 0 else (li <= si)
        one_hot = lambda m: jnp.where(m, 1.0, 0.0).astype(BF16)
        tri = one_hot(keep)
        tri_t = one_hot((si >= li) if d == 0 else (si <= li))
        expand = one_hot(e_row == d * HEADS_PER_GROUP + e_col // SSD_HEADDIM)
        last = CHUNK - 1 if d == 0 else 0

        if has_init:
            s0 = init_ref[0, d].reshape(GROUP_W, D_STATE).T
        else:
            s0 = jnp.zeros((D_STATE, GROUP_W), F32)

        def chunk_body(step, state, d=d, keep=keep, tri=tri, tri_t=tri_t, expand=expand, last=last):
            c = step if d == 0 else n_chunks - 1 - step
            rows = pl.ds(pl.multiple_of(c * CHUNK, CHUNK), CHUNK)
            x_c = xs_s[rows, :]
            b_f = bm_s[rows, :]
            b_c = b_f.astype(BF16)
            c_c = cm_s[rows, :].astype(BF16)
            dt_c = dt_s[rows, :]
            cs = _dot3_rhs(tri, dt_c * a_row)
            cs_t = _dot3_lhs(dtt_s[c] * a_col, tri_t)
            total = cs[last:last + 1, :]
            xdt = x_c * _dot3_lhs(dt_c, expand)
            xdte =(xdt * _dot3_lhs(jnp.exp(total - cs), expand)).astype(BF16)
            cb = lax.dot_general(c_c, b_c, (((1,), (1,)), ((), ())),
                                 preferred_element_type=F32)
            y_off = jnp.dot(c_c, state.astype(BF16), preferred_element_type=F32)
            y_off = y_off * _dot3_lhs(jnp.exp(cs), expand)
            y_diag = []
            for pair in range(HEADS_PER_GROUP // 2):
                mats = []
                for r in (2 * pair, 2 * pair + 1):
                    j = d * HEADS_PER_GROUP + r
                    diff = cs[:, j:j + 1] - cs_t[j:j + 1, :]
                    decay = jnp.exp(jnp.where(keep, diff, -jnp.inf))
                    mats.append((cb * decay).astype(BF16))
                lhs = jnp.concatenate(mats, axis=1)
                blk = xdt[:, pair * LANES:(pair + 1) * LANES]
                rhs = jnp.concatenate([jnp.where(lane < SSD_HEADDIM, blk, 0.0),
                                       jnp.where(lane >= SSD_HEADDIM, blk, 0.0)], axis=0)
                y_diag.append(jnp.dot(lhs, rhs.astype(BF16), preferred_element_type=F32))
            y_c = jnp.concatenate(y_diag, axis=1) + y_off
            if d == 0:
                y_s[rows, :] = y_c
            else:
                y_s[rows, :] = y_s[rows, :] + y_c
            new = jnp.dot(b_f.T.astype(BF16), xdte, preferred_element_type=F32)
            etot = jnp.broadcast_to(jnp.exp(total), (8, n_dh))
            return state * _dot3_lhs(etot, expand)[0:1, :] + new

        state = lax.fori_loop(0, n_chunks, chunk_body, s0)
        fin_ref[0, d] = state.T.reshape(HEADS_PER_GROUP, SSD_HEADDIM, D_STATE)

    y_ref[...] = (y_s[...] + dsk_ref[...] * xs_s[...]).astype(y_ref.dtype)


def _ssd(xbc, conv_w, conv_b, dt_rows, dt_cols, bias_r, bias_c, a_r, a_c, dsk, init,
         *, n_batch, seq_len, row0):
    sb0 = row0 // seq_len
    cb0 = row0 // CHUNK
    n_chunks = seq_len // CHUNK
    n_dh = 2 * HEADS_PER_GROUP
    xw = GROUP_W // D_STATE
    b_blk0 = D_INNER // D_STATE
    c_blk0 = (D_INNER + SSD_GROUPS * D_STATE) // D_STATE
    has_init = init is not None
    in_specs = [
        pl.BlockSpec((seq_len, GROUP_W), lambda b, g: (sb0 + b, g)),
        pl.BlockSpec((seq_len, D_STATE), lambda b, g: (sb0 + b, b_blk0 + g)),
        pl.BlockSpec((seq_len, D_STATE), lambda b, g: (sb0 + b, c_blk0 + g)),
        pl.BlockSpec((CONV_W, GROUP_W), lambda b, g: (0, g)),
        pl.BlockSpec((CONV_W, D_STATE), lambda b, g: (0, b_blk0 + g)),
        pl.BlockSpec((CONV_W, D_STATE), lambda b, g: (0, c_blk0 + g)),
        pl.BlockSpec((1, GROUP_W), lambda b, g: (0, g)),
        pl.BlockSpec((1, D_STATE), lambda b, g: (0, b_blk0 + g)),
        pl.BlockSpec((1, D_STATE), lambda b, g: (0, c_blk0 + g)),
        pl.BlockSpec((1, seq_len, n_dh), lambda b, g: (g, sb0 + b, 0)),
        pl.BlockSpec((1, n_chunks, n_dh, CHUNK), lambda b, g: (g, sb0 + b, 0, 0)),
        pl.BlockSpec((1, 1, n_dh), lambda b, g: (g, 0, 0)),
        pl.BlockSpec((1, n_dh, 1), lambda b, g: (g, 0, 0)),
        pl.BlockSpec((1, 1, n_dh), lambda b, g: (g, 0, 0)),
        pl.BlockSpec((1, n_dh, 1), lambda b, g: (g, 0, 0)),
        pl.BlockSpec((1, GROUP_W), lambda b, g: (0, g)),
    ]
    del xw, cb0
    args = [xbc, xbc, xbc, conv_w, conv_w, conv_w, conv_b, conv_b, conv_b,
            dt_rows, dt_cols, bias_r, bias_c, a_r, a_c, dsk]
    state_spec = pl.BlockSpec((1, 2, HEADS_PER_GROUP, SSD_HEADDIM, D_STATE),
                              lambda b, g: (b, 0, g, 0, 0))
    if has_init:
        in_specs.append(state_spec)
        args.append(init)
    return pl.pallas_call(
        functools.partial(_ssd_kernel, seq_len=seq_len, has_init=has_init),
        grid=(n_batch, SSD_GROUPS),
        in_specs=in_specs,
        out_specs=[pl.BlockSpec((seq_len, GROUP_W), lambda b, g: (b, g)), state_spec],
        out_shape=[jax.ShapeDtypeStruct((n_batch * seq_len, D_INNER), BF16),
                   jax.ShapeDtypeStruct((n_batch, 2, SSD_HEADS, SSD_HEADDIM, D_STATE), F32)],
        scratch_shapes=[pltpu.VMEM((seq_len, GROUP_W), F32),
                        pltpu.VMEM((seq_len, D_STATE), F32),
                        pltpu.VMEM((seq_len, D_STATE), F32),
                        pltpu.VMEM((seq_len, n_dh), F32),
                        pltpu.VMEM((n_chunks, n_dh, CHUNK), F32),
                        pltpu.VMEM((seq_len, GROUP_W), F32)],
        compiler_params=_params("arbitrary", "arbitrary"),
        name="ssd",
    )(*args)


def _ossd_kernel(y_ref, z_ref, ng_ref, w_ref, g_ref, om_ref, o_ref, a_s):
    @pl.when(pl.program_id(1) == 0)
    def _():
        for r0 in range(0, a_s.shape[0], CHUNK):
            rows = slice(r0, r0 + CHUNK)
            u = y_ref[rows, :].astype(F32) * _silu(z_ref[rows, :].astype(F32))
            a_s[rows, :] = (_rms(u) * ng_ref[...]).astype(a_s.dtype)

    p = jnp.dot(a_s[...], w_ref[...], preferred_element_type=F32)
    o_ref[...] = (p * g_ref[...].astype(F32) + om_ref[...]).astype(o_ref.dtype)


def _ossd(y, z, norm_g, w, gates, omla):
    tm, tn = 512, 1024
    g_blk0 = D_MODEL // tn
    return pl.pallas_call(
        _ossd_kernel,
        grid=(N_TOK // tm, D_MODEL // tn),
        in_specs=[pl.BlockSpec((tm, D_INNER), lambda i, j: (i, 0)),
                  pl.BlockSpec((tm, D_INNER), lambda i, j: (i, 0)),
                  pl.BlockSpec((1, D_INNER), lambda i, j: (0, 0)),
                  pl.BlockSpec((D_INNER, tn), lambda i, j: (0, j)),
                  pl.BlockSpec((tm, tn), lambda i, j: (i, g_blk0 + j)),
                  pl.BlockSpec((tm, tn), lambda i, j: (i, j))],
        out_specs=pl.BlockSpec((tm, tn), lambda i, j: (i, j)),
        out_shape=jax.ShapeDtypeStruct((N_TOK, D_MODEL), BF16),
        scratch_shapes=[pltpu.VMEM((tm, D_INNER), BF16)],
        compiler_params=_params("arbitrary", "arbitrary"),
        name="ossd_merge",
    )(y, z, norm_g.reshape(1, D_INNER), w, gates, omla)


def _outproj_kernel(a_ref, w_ref, x_ref, pg_ref, g1_ref, p2_ref, sc_ref, sh_ref, wr_ref,
                    x1_ref, h2_ref, lg_ref):
    out = jnp.dot(a_ref[...], w_ref[...], preferred_element_type=F32)
    x1 = x_ref[...] + g1_ref[0] * (_rms(out) * pg_ref[...])
    x1_ref[...] = x1
    h2 = ((_rms(x1) * p2_ref[...]) * (1.0 + sc_ref[0]) + sh_ref[0]).astype(BF16)
    h2_ref[...] = h2
    lg_ref[...] = jnp.dot(h2, wr_ref[...], preferred_element_type=F32)


def _outproj(merged, w_out, x, post1_g, pre2_g, mod, w_router_pad):
    tm = 256
    r = _mod_row(tm)
    row = lambda i: (i, 0)
    one = lambda i: (0, 0)
    return pl.pallas_call(
        _outproj_kernel,
        grid=(N_TOK // tm,),
        in_specs=[pl.BlockSpec((tm, D_MODEL), row),
                  pl.BlockSpec((D_MODEL, D_MODEL), one),
                  pl.BlockSpec((tm, D_MODEL), row),
                  pl.BlockSpec((1, D_MODEL), one),
                  pl.BlockSpec((1, 1, D_MODEL), lambda i: (r(i), 0, 2)),
                  pl.BlockSpec((1, D_MODEL), one),
                  pl.BlockSpec((1, 1, D_MODEL), lambda i: (r(i), 0, 4)),
                  pl.BlockSpec((1, 1, D_MODEL), lambda i: (r(i), 0, 3)),
                  pl.BlockSpec((D_MODEL, LANES), one)],
        out_specs=[pl.BlockSpec((tm, D_MODEL), row),
                   pl.BlockSpec((tm, D_MODEL), row),
                   pl.BlockSpec((tm, LANES), row)],
        out_shape=[jax.ShapeDtypeStruct((N_TOK, D_MODEL), F32),
                   jax.ShapeDtypeStruct((N_TOK, D_MODEL), BF16),
                   jax.ShapeDtypeStruct((N_TOK, LANES), F32)],
        compiler_params=_params("arbitrary"),
        name="outproj",
    )(merged, w_out, x, post1_g.reshape(1, D_MODEL), mod, pre2_g.reshape(1, D_MODEL), mod, mod,
      w_router_pad)


def _ffn_kernel(x_ref, wg_ref, wu_ref, wd_ref, gv_ref, o_ref, acc_ref):
    f = pl.program_id(1)
    wg = wg_ref[0].astype(BF16)
    wu = wu_ref[0].astype(BF16)
    wd = wd_ref[0].astype(BF16)
    n_rows = x_ref.shape[1]
    for r0 in range(0, n_rows, FFN_ROWS):
        rows = slice(r0, r0 + FFN_ROWS)
        x = x_ref[0, rows, :]
        g = jnp.dot(x, wg, preferred_element_type=F32)
        u = jnp.dot(x, wu, preferred_element_type=F32)
        hid = (_silu(g) * u).astype(BF16)
        part = jnp.dot(hid, wd, preferred_element_type=F32)

        @pl.when(f == 0)
        def _():
            acc_ref[rows, :] = part

        @pl.when(f > 0)
        def _():
            acc_ref[rows, :] += part

    @pl.when(f == pl.num_programs(1) - 1)
    def _():
        o_ref[0] = (acc_ref[...] * gv_ref[0]).astype(o_ref.dtype)


def _ffn(xe, w_gate, w_up, w_down, gval):
    n_rows = xe.shape[1]
    tf = 256
    return pl.pallas_call(
        _ffn_kernel,
        grid=(N_EXPERTS, D_EXPERT // tf),
        in_specs=[pl.BlockSpec((1, n_rows, D_MODEL), lambda e, f: (e, 0, 0),
                               pipeline_mode=pl.Buffered(1)),
                  pl.BlockSpec((1, D_MODEL, tf), lambda e, f: (e, 0, f)),
                  pl.BlockSpec((1, D_MODEL, tf), lambda e, f: (e, 0, f)),
                  pl.BlockSpec((1, tf, D_MODEL), lambda e, f: (e, f, 0)),
                  pl.BlockSpec((1, n_rows, 1), lambda e, f: (e, 0, 0))],
        out_specs=pl.BlockSpec((1, n_rows, D_MODEL), lambda e, f: (e, 0, 0)),
        out_shape=jax.ShapeDtypeStruct((N_EXPERTS, n_rows, D_MODEL), BF16),
        scratch_shapes=[pltpu.VMEM((n_rows, D_MODEL), F32)],
        compiler_params=_params("arbitrary", "arbitrary"),
        name="expert_ffn",
    )(xe, w_gate, w_up, w_down, gval)


def _final_kernel(x_ref, m_ref, pg_ref, g2_ref, o_ref):
    o_ref[...] = x_ref[...] + g2_ref[0] * (_rms(m_ref[...]) * pg_ref[...])


def _final(x1, moe, post2_g, mod):
    tm = 256
    r = _mod_row(tm)
    row = lambda i: (i, 0)
    return pl.pallas_call(
        _final_kernel,
        grid=(N_TOK // tm,),
        in_specs=[pl.BlockSpec((tm, D_MODEL), row),
                  pl.BlockSpec((tm, D_MODEL), row),
                  pl.BlockSpec((1, D_MODEL), lambda i: (0, 0)),
                  pl.BlockSpec((1, 1, D_MODEL), lambda i: (r(i), 0, 5))],
        out_specs=pl.BlockSpec((tm, D_MODEL), row),
        out_shape=jax.ShapeDtypeStruct((N_TOK, D_MODEL), F32),
        compiler_params=_params("arbitrary"),
        name="final",
    )(x1, moe, post2_g.reshape(1, D_MODEL), mod)


def _rope_tables():
    rows = DEC_SEQ // GRID_W
    row = jnp.repeat(jnp.arange(rows), GRID_W).astype(F32)
    col = jnp.tile(jnp.arange(GRID_W), rows).astype(F32)
    half = QK_ROPE // 2
    inv = ROPE_THETA ** (-jnp.arange(0, half, 2, dtype=F32) / half)
    ang_r = row[:, None] * inv[None, :]
    ang_c = col[:, None] * inv[None, :]
    ang = jnp.concatenate([ang_r, ang_r, ang_c, ang_c], axis=-1)
    cos, sin = jnp.cos(ang), jnp.sin(ang)
    ident = 256
    one = lambda n: jnp.ones((DEC_SEQ, n), F32)
    zero = lambda n: jnp.zeros((DEC_SEQ, n), F32)

    def table(parts_cos, parts_sin, width):
        c = jnp.concatenate([jnp.ones((ident, width), F32), jnp.concatenate(parts_cos, -1)], 0)
        s = jnp.concatenate([jnp.zeros((ident, width), F32), jnp.concatenate(parts_sin, -1)], 0)
        return c, s

    cos_k, sin_k = table([cos, one(LANES - QK_ROPE)], [sin, zero(LANES - QK_ROPE)], LANES)
    pad = Q_HEAD_PAD - QK_NOPE - QK_ROPE
    cos_q, sin_q = table([one(QK_NOPE), cos, one(pad)], [zero(QK_NOPE), sin, zero(pad)], Q_HEAD_PAD)
    return cos_k, sin_k, cos_q, sin_q


def _route(logits, n_tok):
    cap = CAP_FACTOR * n_tok // N_EXPERTS
    aff = jax.nn.softmax(logits, axis=-1)
    return lax.top_k(aff.T, cap)


def kernel(x_prompt, x_sample, cache_ckv, cache_krope, state_ssm, c, c_ctx, w_mod, b_mod,
           pre1_g, post1_g, pre2_g, post2_g, w_in, q_norm_g, w_q_b, kv_norm_g, w_kv_b, w_o_mla,
           conv_w, conv_b, dt_bias, a_log, d_skip, ssd_norm_g, w_o_ssd, w_out, w_router,
           w_e_gate, w_e_up, w_e_down):
    l = 0
    x = jnp.concatenate([x_prompt.reshape(N_CTX, D_MODEL), x_sample.reshape(N_LAT, D_MODEL)], 0)

    n_mod = 1 + DEC_BATCH
    cvec = jnp.concatenate([c_ctx[None, :], c, jnp.zeros((16 - n_mod, D_MODEL), F32)], 0)
    mod = _adaln(cvec, w_mod[l], b_mod[l]).reshape(16, 1, 6 * D_MODEL)

    wi = w_in[l]
    o_z = Q_RANK + KV_RANK + QK_ROPE
    o_xbc = o_z + D_INNER
    o_dt = o_xbc + XBC_DIM
    o_gate = o_dt + 2 * SSD_HEADS
    w_small = jnp.concatenate([wi[:, :o_z], jnp.zeros((D_MODEL, LANES - QK_ROPE), F32),
                               wi[:, o_dt:o_gate]], axis=1).astype(BF16)
    w_z = wi[:, o_z:o_xbc].astype(BF16)
    w_xbc = wi[:, o_xbc:o_dt].astype(BF16)
    w_gates = wi[:, o_gate:].astype(BF16)
    wq = w_q_b[l].reshape(Q_RANK, N_HEADS, QK_NOPE + QK_ROPE)
    wq = jnp.concatenate([wq, jnp.zeros((Q_RANK, N_HEADS, Q_HEAD_PAD - QK_NOPE - QK_ROPE), F32)],
                         axis=-1).reshape(Q_RANK, N_HEADS * Q_HEAD_PAD).astype(BF16)
    cos_k, sin_k, cos_q, sin_q = _rope_tables()

    h1 = _prenorm(x, pre1_g[l], mod, 1, 0)
    cqn, ckv, kr, dt_raw = _inproj_small(h1, w_small, q_norm_g[l], kv_norm_g[l], cos_k, sin_k)
    z = _mm(h1, w_z, tm=512, tn=1024, out_dtype=BF16, name="inproj_z")
    xbc = _mm(h1, w_xbc, tm=512, tn=1024, out_dtype=BF16, name="inproj_xbc")
    gates = _mm(h1, w_gates, tm=512, tn=1024, out_dtype=BF16, act="sigmoid", name="inproj_gates")

    q = _qproj(cqn, wq, cos_q, sin_q)
    ckv_b = ckv.astype(BF16)
    kr_b = kr.astype(BF16)
    keys_ctx = ckv_b[:N_CTX]
    keys_lat = jnp.concatenate([cache_ckv[:, l].astype(BF16),
                                ckv_b[N_CTX:].reshape(DEC_BATCH, DEC_SEQ, KV_RANK)], axis=1)
    keys_lat = keys_lat.reshape(DEC_BATCH * (PAST_LEN + DEC_SEQ), KV_RANK)
    kr_cache = jnp.concatenate([cache_krope[:, l].astype(BF16),
                                jnp.zeros((DEC_BATCH, PAST_LEN, LANES - QK_ROPE), BF16)], axis=-1)
    kr_lat = jnp.concatenate([kr_cache, kr_b[N_CTX:].reshape(DEC_BATCH, DEC_SEQ, LANES)], axis=1)
    kr_lat = kr_lat.reshape(DEC_BATCH * (PAST_LEN + DEC_SEQ), LANES)
    wkv = w_kv_b[l].astype(BF16)
    kv_ctx = _mm(keys_ctx, wkv, tm=512, tn=2048, out_dtype=BF16, name="kv_ctx")
    kv_lat = _mm(keys_lat, wkv, tm=512, tn=2048, out_dtype=BF16, name="kv_lat")
    o_ctx = _attention(q, kv_ctx, kr_b[:N_CTX], n_batch=BATCH, lq=SEQ, lk=SEQ, tq=SEQ, q_row0=0)
    o_lat = _attention(q, kv_lat, kr_lat, n_batch=DEC_BATCH, lq=DEC_SEQ, lk=PAST_LEN + DEC_SEQ,
                       tq=512, q_row0=N_CTX)
    attn_o = jnp.concatenate([o_ctx, o_lat], axis=0)
    omla = _omla(attn_o, w_o_mla[l].astype(BF16), gates)

    n_dh = 2 * HEADS_PER_GROUP
    dt4 = dt_raw.reshape(N_TOK, 2, SSD_GROUPS, HEADS_PER_GROUP)
    dt_rows = dt4.transpose(2, 0, 1, 3).reshape(SSD_GROUPS, N_TOK, n_dh)
    dt_cols = dt_rows.reshape(SSD_GROUPS, N_TOK // CHUNK, CHUNK, n_dh).transpose(0, 1, 3, 2)
    per_group = lambda v: v.reshape(2, SSD_GROUPS, HEADS_PER_GROUP).transpose(1, 0, 2).reshape(
        SSD_GROUPS, n_dh)
    bias_g = per_group(dt_bias[l])
    a_g = per_group(-jnp.exp(a_log[l]))
    dsk = jnp.repeat(d_skip[l], SSD_HEADDIM).reshape(1, D_INNER)
    ssd_args = (xbc, conv_w[l], conv_b[l].reshape(1, XBC_DIM), dt_rows, dt_cols,
                bias_g[:, None, :], bias_g[:, :, None], a_g[:, None, :], a_g[:, :, None], dsk)
    y_ctx, ssm_ctx = _ssd(*ssd_args, None, n_batch=BATCH, seq_len=SEQ, row0=0)
    y_lat, _ = _ssd(*ssd_args, state_ssm[:, l], n_batch=DEC_BATCH, seq_len=DEC_SEQ, row0=N_CTX)
    y = jnp.concatenate([y_ctx, y_lat], axis=0)
    merged = _ossd(y, z, ssd_norm_g[l], w_o_ssd[l].astype(BF16), gates, omla)

    w_router_pad = jnp.concatenate([w_router[l], jnp.zeros((D_MODEL, LANES - N_EXPERTS), F32)],
                                   axis=1).astype(BF16)
    x1, h2, logits = _outproj(merged, w_out[l].astype(BF16), x, post1_g[l], pre2_g[l], mod,
                              w_router_pad)

    logits = logits[:, :N_EXPERTS]
    gv_c, idx_c = _route(logits[:N_CTX], N_CTX)
    gv_l, idx_l = _route(logits[N_CTX:], N_LAT)
    idx = jnp.concatenate([idx_c, idx_l + N_CTX], axis=1)
    gval = jnp.concatenate([gv_c, gv_l], axis=1)
    xe = h2[idx]
    ye = _ffn(xe, w_e_gate[l], w_e_up[l], w_e_down[l], gval[..., None])
    moe = jnp.zeros((N_TOK, D_MODEL), F32).at[idx.reshape(-1)].add(
        ye.reshape(-1, D_MODEL).astype(F32))
    out = _final(x1, moe, post2_g[l], mod)

    y_p = out[:N_CTX].reshape(BATCH, SEQ, D_MODEL)
    y_s = out[N_CTX:].reshape(DEC_BATCH, DEC_SEQ, D_MODEL)
    new_ckv = ckv[:N_CTX].reshape(BATCH, 1, SEQ, KV_RANK)
    new_krope = kr[:N_CTX, :QK_ROPE].reshape(BATCH, 1, SEQ, QK_ROPE)
    new_ssm = ssm_ctx[:, None]
    return (y_p, y_s, new_ckv, new_krope, new_ssm)
```

```python
import functools

import jax
import jax.numpy as jnp
import numpy as np
from jax import lax
from jax.experimental import pallas as pl
from jax.experimental.pallas import tpu as pltpu

F32 = jnp.float32
BF16 = jnp.bfloat16

D_MODEL = 2048
BATCH = 16
SEQ = 256
DEC_BATCH = 8
DEC_SEQ = 1024
PAST_LEN = 512
GRID_W = 64
N_HEADS = 16
Q_RANK = 512
KV_RANK = 512
QK_NOPE = 128
QK_ROPE = 64
V_DIM = 128
ROPE_THETA = 10000.0
D_INNER = 2 * D_MODEL
SSD_HEADDIM = 64
SSD_HEADS = D_INNER // SSD_HEADDIM
SSD_GROUPS = 8
HEADS_PER_GROUP = SSD_HEADS // SSD_GROUPS
D_STATE = 128
CONV_W = 5
CHUNK = 128
XBC_DIM = D_INNER + 2 * SSD_GROUPS * D_STATE
N_EXPERTS = 16
CAP_FACTOR = 2
D_EXPERT = 1536
EPS = 1e-6

N_CTX = BATCH * SEQ
N_LAT = DEC_BATCH * DEC_SEQ
N_TOK = N_CTX + N_LAT
Q_HEAD_PAD = 256
GROUP_W = HEADS_PER_GROUP * SSD_HEADDIM
LANES = 128
VMEM_LIMIT = 56 * 1024 * 1024
CONV_HALO = 8
FFN_ROWS = 512


def _params(*sem):
    return pltpu.CompilerParams(dimension_semantics=sem, vmem_limit_bytes=VMEM_LIMIT)


def _rms(x):
    return x * lax.rsqrt(jnp.mean(x * x, axis=-1, keepdims=True) + EPS)


def _silu(x):
    return x * jax.nn.sigmoid(x)


def _mod_row(tm):
    n_ctx_tiles = N_CTX // tm
    per_lat = DEC_SEQ // tm
    return lambda i: jnp.where(i < n_ctx_tiles, 0, 1 + (i - n_ctx_tiles) // per_lat)


def _rope_row(tm):
    n_ctx_tiles = N_CTX // tm
    per_lat = DEC_SEQ // tm
    return lambda i: jnp.where(i < n_ctx_tiles, 0, 1 + (i - n_ctx_tiles) % per_lat)


def _rot_half(x):
    w = x.shape[1]
    nxt = pltpu.roll(x, w - 16, axis=1)
    prv = pltpu.roll(x, 16, axis=1)
    lane = lax.broadcasted_iota(jnp.int32, x.shape, 1)
    return jnp.where((lane % 32) < 16, -nxt, prv)


def _adaln_kernel(c_ref, w_ref, b_ref, o_ref):
    a = _silu(c_ref[...]).astype(BF16)
    o_ref[...] = jnp.dot(a, w_ref[...].astype(BF16), preferred_element_type=F32) + b_ref[...]


def _adaln(cvec, w_mod, b_mod):
    rows = cvec.shape[0]
    n = w_mod.shape[1]
    tn = 1024
    return pl.pallas_call(
        _adaln_kernel,
        grid=(n // tn,),
        in_specs=[pl.BlockSpec((rows, D_MODEL), lambda j: (0, 0)),
                  pl.BlockSpec((D_MODEL, tn), lambda j: (0, j)),
                  pl.BlockSpec((1, tn), lambda j: (0, j))],
        out_specs=pl.BlockSpec((rows, tn), lambda j: (0, j)),
        out_shape=jax.ShapeDtypeStruct((rows, n), F32),
        compiler_params=_params("arbitrary"),
        name="adaln",
    )(cvec, w_mod, b_mod.reshape(1, n))


def _prenorm_kernel(x_ref, g_ref, sc_ref, sh_ref, o_ref):
    y = _rms(x_ref[...]) * g_ref[...]
    o_ref[...] = (y * (1.0 + sc_ref[0]) + sh_ref[0]).astype(o_ref.dtype)


def _prenorm(x, gain, mod, sc_blk, sh_blk):
    tm = 256
    r = _mod_row(tm)
    return pl.pallas_call(
        _prenorm_kernel,
        grid=(N_TOK // tm,),
        in_specs=[pl.BlockSpec((tm, D_MODEL), lambda i: (i, 0)),
                  pl.BlockSpec((1, D_MODEL), lambda i: (0, 0)),
                  pl.BlockSpec((1, 1, D_MODEL), lambda i: (r(i), 0, sc_blk)),
                  pl.BlockSpec((1, 1, D_MODEL), lambda i: (r(i), 0, sh_blk))],
        out_specs=pl.BlockSpec((tm, D_MODEL), lambda i: (i, 0)),
        out_shape=jax.ShapeDtypeStruct((N_TOK, D_MODEL), BF16),
        compiler_params=_params("arbitrary"),
        name="prenorm",
    )(x, gain.reshape(1, D_MODEL), mod, mod)


def _inproj_small_kernel(h_ref, w_ref, qg_ref, kvg_ref, cos_ref, sin_ref,
                         cqn_ref, ckv_ref, kr_ref, dt_ref):
    p = jnp.dot(h_ref[...], w_ref[...], preferred_element_type=F32)
    cqn_ref[...] = (_rms(p[:, :Q_RANK]) * qg_ref[...]).astype(cqn_ref.dtype)
    ckv_ref[...] = _rms(p[:, Q_RANK:Q_RANK + KV_RANK]) * kvg_ref[...]
    kr = p[:, Q_RANK + KV_RANK:Q_RANK + KV_RANK + LANES]
    kr_ref[...] = kr * cos_ref[...] + _rot_half(kr) * sin_ref[...]
    dt_ref[...] = p[:, Q_RANK + KV_RANK + LANES:]


def _inproj_small(h, w, q_norm_g, kv_norm_g, cos_k, sin_k):
    tm = 256
    n = w.shape[1]
    rr = _rope_row(tm)
    row = lambda i: (i, 0)
    return pl.pallas_call(
        _inproj_small_kernel,
        grid=(N_TOK // tm,),
        in_specs=[pl.BlockSpec((tm, D_MODEL), row),
                  pl.BlockSpec((D_MODEL, n), lambda i: (0, 0)),
                  pl.BlockSpec((1, Q_RANK), lambda i: (0, 0)),
                  pl.BlockSpec((1, KV_RANK), lambda i: (0, 0)),
                  pl.BlockSpec((tm, LANES), lambda i: (rr(i), 0)),
                  pl.BlockSpec((tm, LANES), lambda i: (rr(i), 0))],
        out_specs=[pl.BlockSpec((tm, Q_RANK), row),
                   pl.BlockSpec((tm, KV_RANK), row),
                   pl.BlockSpec((tm, LANES), row),
                   pl.BlockSpec((tm, LANES), row)],
        out_shape=[jax.ShapeDtypeStruct((N_TOK, Q_RANK), BF16),
                   jax.ShapeDtypeStruct((N_TOK, KV_RANK), F32),
                   jax.ShapeDtypeStruct((N_TOK, LANES), F32),
                   jax.ShapeDtypeStruct((N_TOK, LANES), F32)],
        compiler_params=_params("arbitrary"),
        name="inproj_small",
    )(h, w, q_norm_g.reshape(1, Q_RANK), kv_norm_g.reshape(1, KV_RANK), cos_k, sin_k)


def _mm_kernel(a_ref, w_ref, o_ref, *, act):
    p = jnp.dot(a_ref[...], w_ref[...], preferred_element_type=F32)
    if act == "sigmoid":
        p = jax.nn.sigmoid(p)
    o_ref[...] = p.astype(o_ref.dtype)


def _mm(a, w, *, tm, tn, out_dtype, act=None, name="mm"):
    m, k = a.shape
    n = w.shape[1]
    return pl.pallas_call(
        functools.partial(_mm_kernel, act=act),
        grid=(n // tn, m // tm),
        in_specs=[pl.BlockSpec((tm, k), lambda j, i: (i, 0)),
                  pl.BlockSpec((k, tn), lambda j, i: (0, j))],
        out_specs=pl.BlockSpec((tm, tn), lambda j, i: (i, j)),
        out_shape=jax.ShapeDtypeStruct((m, n), out_dtype),
        compiler_params=_params("arbitrary", "arbitrary"),
        name=name,
    )(a, w)


def _qproj_kernel(a_ref, w_ref, cos_ref, sin_ref, o_ref):
    p = jnp.dot(a_ref[...], w_ref[...], preferred_element_type=F32)
    cos = cos_ref[...]
    sin = sin_ref[...]
    for h in range(N_HEADS):
        sl = slice(h * Q_HEAD_PAD, (h + 1) * Q_HEAD_PAD)
        ph = p[:, sl]
        o_ref[:, sl] = (ph * cos + _rot_half(ph) * sin).astype(o_ref.dtype)


def _qproj(cqn, wq, cos_q, sin_q):
    tm = 256
    n = wq.shape[1]
    rr = _rope_row(tm)
    return pl.pallas_call(
        _qproj_kernel,
        grid=(N_TOK // tm,),
        in_specs=[pl.BlockSpec((tm, Q_RANK), lambda i: (i, 0)),
                  pl.BlockSpec((Q_RANK, n), lambda i: (0, 0)),
                  pl.BlockSpec((tm, Q_HEAD_PAD), lambda i: (rr(i), 0)),
                  pl.BlockSpec((tm, Q_HEAD_PAD), lambda i: (rr(i), 0))],
        out_specs=pl.BlockSpec((tm, n), lambda i: (i, 0)),
        out_shape=jax.ShapeDtypeStruct((N_TOK, n), BF16),
        compiler_params=_params("arbitrary"),
        name="qproj",
    )(cqn, wq, cos_q, sin_q)


ATTN_Q_ROWS = 128


def _attn_kernel(q_ref, kv_ref, kr_ref, o_ref, *, heads):
    kr = kr_ref[...]
    for h in range(heads):
        k = jnp.concatenate([kv_ref[:, 2 * h * LANES:(2 * h + 1) * LANES], kr], axis=1)
        v = kv_ref[:, (2 * h + 1) * LANES:(2 * h + 2) * LANES]
        for r0 in range(0, q_ref.shape[0], ATTN_Q_ROWS):
            rows = slice(r0, r0 + ATTN_Q_ROWS)
            q = q_ref[rows, h * Q_HEAD_PAD:(h + 1) * Q_HEAD_PAD]
            s = lax.dot_general(q, k, (((1,), (1,)), ((), ())), preferred_element_type=F32)
            p = jnp.exp2(s - jnp.max(s, axis=-1, keepdims=True))
            l = jnp.sum(p, axis=-1, keepdims=True)
            o = jnp.dot(p.astype(BF16), v, preferred_element_type=F32)
            o_ref[rows, h * V_DIM:(h + 1) * V_DIM] = (o / l).astype(o_ref.dtype)


def _attention(q, kv, kr, *, n_batch, lq, lk, tq, heads, q_row0):
    nq = lq // tq
    qb0 = q_row0 // tq
    return pl.pallas_call(
        functools.partial(_attn_kernel, heads=heads),
        grid=(n_batch, N_HEADS // heads, nq),
        in_specs=[pl.BlockSpec((tq, heads * Q_HEAD_PAD), lambda b, h, i: (qb0 + b * nq + i, h)),
                  pl.BlockSpec((lk, heads * (QK_NOPE + V_DIM)), lambda b, h, i: (b, h)),
                  pl.BlockSpec((lk, LANES), lambda b, h, i: (b, 0))],
        out_specs=pl.BlockSpec((tq, heads * V_DIM), lambda b, h, i: (b * nq + i, h)),
        out_shape=jax.ShapeDtypeStruct((n_batch * lq, N_HEADS * V_DIM), BF16),
        compiler_params=_params("arbitrary", "arbitrary", "arbitrary"),
        name="attention",
    )(q, kv, kr)


def _split_rows_specs(tm, width, n_ctx_tiles):
    ctx = pl.BlockSpec((tm, width), lambda j, i: (jnp.minimum(i, n_ctx_tiles - 1), 0))
    lat = pl.BlockSpec((tm, width), lambda j, i: (jnp.maximum(i - n_ctx_tiles, 0), 0))
    return ctx, lat


def _omla_kernel(ac_ref, al_ref, w_ref, g_ref, o_ref, *, n_ctx_tiles):
    def run(a_ref):
        p = jnp.dot(a_ref[...], w_ref[...], preferred_element_type=F32)
        o_ref[...] = p * g_ref[...].astype(F32)

    i = pl.program_id(1)
    pl.when(i < n_ctx_tiles)(lambda: run(ac_ref))
    pl.when(i >= n_ctx_tiles)(lambda: run(al_ref))


def _omla(o_ctx, o_lat, w, gates):
    tm, tn = 512, 1024
    n_ctx_tiles = N_CTX // tm
    ctx, lat = _split_rows_specs(tm, D_MODEL, n_ctx_tiles)
    return pl.pallas_call(
        functools.partial(_omla_kernel, n_ctx_tiles=n_ctx_tiles),
        grid=(D_MODEL // tn, N_TOK // tm),
        in_specs=[ctx, lat,
                  pl.BlockSpec((D_MODEL, tn), lambda j, i: (0, j)),
                  pl.BlockSpec((tm, tn), lambda j, i: (i, j))],
        out_specs=pl.BlockSpec((tm, tn), lambda j, i: (i, j)),
        out_shape=jax.ShapeDtypeStruct((N_TOK, D_MODEL), F32),
        compiler_params=_params("arbitrary", "arbitrary"),
        name="omla",
    )(o_ctx, o_lat, w, gates)


def _split3(x):
    hi = x.astype(BF16)
    r1 = x - hi.astype(F32)
    mid = r1.astype(BF16)
    lo = (r1 - mid.astype(F32)).astype(BF16)
    return hi, mid, lo


def _dot3_lhs(x, sel):
    return sum(jnp.dot(p, sel, preferred_element_type=F32) for p in _split3(x))


def _dot3_rhs(sel, x):
    return sum(jnp.dot(sel, p, preferred_element_type=F32) for p in _split3(x))


def _softplus(x):
    return jnp.maximum(x, 0.0) + jnp.log1p(jnp.exp(-jnp.abs(x)))


def _conv_silu_chunk(pad_ref, w_ref, b_ref, chunk):
    base = CONV_HALO + chunk * CHUNK - CONV_W // 2
    acc = b_ref[...] + w_ref[0:1, :] * pad_ref[base:base + CHUNK, :]
    for k in range(1, CONV_W):
        acc = acc + w_ref[k:k + 1, :] * pad_ref[base + k:base + k + CHUNK, :]
    return _silu(acc)


def _head_pair_blockdiag(blk, lane):
    return jnp.concatenate([jnp.where(lane < SSD_HEADDIM, blk, 0.0),
                            jnp.where(lane >= SSD_HEADDIM, blk, 0.0)], axis=0)


def _ssd_kernel(*refs, seq_len, has_init):
    if has_init:
        (xs_ref, b_ref, c_ref, wx_ref, wb_ref, wc_ref, bx_ref, bb_ref, bc_ref,
         dt_ref, dtt_ref, bias_r_ref, bias_c_ref, a_r_ref, a_c_ref, dsk_ref, init_ref,
         y_ref, fin_ref, xp_s, bp_s, cp_s, rp_s, bt_s, bb_s, cf_s, dt_s, dtt_s, y_s) = refs
    else:
        (xs_ref, b_ref, c_ref, wx_ref, wb_ref, wc_ref, bx_ref, bb_ref, bc_ref,
         dt_ref, dtt_ref, bias_r_ref, bias_c_ref, a_r_ref, a_c_ref, dsk_ref,
         y_ref, fin_ref, xp_s, bp_s, cp_s, rp_s, bt_s, bb_s, cf_s, dt_s, dtt_s, y_s) = refs
        init_ref = None
    n_chunks = seq_len // CHUNK
    n_pairs = HEADS_PER_GROUP // 2
    lane = lax.broadcasted_iota(jnp.int32, (CHUNK, LANES), 1)

    for pad_ref, src_ref in ((xp_s, xs_ref), (bp_s, b_ref), (cp_s, c_ref)):
        zeros = jnp.zeros((CONV_HALO, pad_ref.shape[1]), F32)
        pad_ref[0:CONV_HALO, :] = zeros
        pad_ref[CONV_HALO + seq_len:, :] = zeros
        for c in range(n_chunks):
            pad_ref[CONV_HALO + c * CHUNK:CONV_HALO + (c + 1) * CHUNK, :] = (
                src_ref[c * CHUNK:(c + 1) * CHUNK, :].astype(F32))

    for c in range(n_chunks):
        rows = slice(c * CHUNK, (c + 1) * CHUNK)
        x_c = _conv_silu_chunk(xp_s, wx_ref, bx_ref, c)
        y_s[rows, :] = dsk_ref[...] * x_c
        for q in range(n_pairs):
            rp_s[c, q] = _head_pair_blockdiag(x_c[:, q * LANES:(q + 1) * LANES], lane).astype(BF16)
        b_c = _conv_silu_chunk(bp_s, wb_ref, bb_ref, c)
        bt_s[c] = b_c.T
        bb_s[c] = b_c.astype(BF16)
        cf_s[c] = _conv_silu_chunk(cp_s, wc_ref, bc_ref, c)
    dt_s[...] = _softplus(dt_ref[0] + bias_r_ref[0])
    dtt_s[...] = _softplus(dtt_ref[0] + bias_c_ref[0])

    li = lax.broadcasted_iota(jnp.int32, (CHUNK, CHUNK), 0)
    si = lax.broadcasted_iota(jnp.int32, (CHUNK, CHUNK), 1)
    one_hot = lambda m: jnp.where(m, 1.0, 0.0).astype(BF16)
    keep = (li >= si, li <= si)
    tri = (one_hot(keep[0]), one_hot(keep[1]))
    tri_t = (one_hot(si >= li), one_hot(si <= li))
    last = (CHUNK - 1, 0)
    a_row = a_r_ref[0]
    a_col = a_c_ref[0]

    def scan_chunk(d, c, state):
        rows = pl.ds(pl.multiple_of(c * CHUNK, CHUNK), CHUNK)
        dt_t = dtt_s[c]
        cs = _dot3_rhs(tri[d], dt_s[rows, :] * a_row)
        cs_t = _dot3_lhs(dt_t * a_col, tri_t[d])
        ecs = jnp.exp(cs)
        etot = ecs[last[d]:last[d] + 1, :]
        w_t = dt_t * jnp.exp(cs_t[:, last[d]:last[d] + 1] - cs_t)
        c_f = cf_s[c]
        cb = lax.dot_general(c_f.astype(BF16), bb_s[c], (((1,), (1,)), ((), ())),
                             preferred_element_type=F32)
        b_t = bt_s[c]
        y_pairs = []
        new_state = []
        for q in range(n_pairs):
            m_l, ce_l, bw_l = [], [], []
            for r in (2 * q, 2 * q + 1):
                j = d * HEADS_PER_GROUP + r
                cs_l = jnp.broadcast_to(cs[:, j:j + 1], (CHUNK, CHUNK))
                decay = jnp.exp(jnp.where(keep[d], cs_l - cs_t[j:j + 1, :], -jnp.inf))
                m_l.append((cb * decay * dt_t[j:j + 1, :]).astype(BF16))
                ce_l.append((c_f * jnp.broadcast_to(ecs[:, j:j + 1], (CHUNK, CHUNK))).astype(BF16))
                bw_l.append((b_t * w_t[j:j + 1, :]).astype(BF16))
            x_rhs = rp_s[c, q]
            s_rhs = _head_pair_blockdiag(state[q], lane).astype(BF16)
            y_pairs.append(jnp.dot(jnp.concatenate(m_l + ce_l, axis=1),
                                   jnp.concatenate([x_rhs, s_rhs], axis=0),
                                   preferred_element_type=F32))
            new = jnp.dot(jnp.concatenate(bw_l, axis=1), x_rhs, preferred_element_type=F32)
            ja = d * HEADS_PER_GROUP + 2 * q
            e_row = jnp.where(lane[0:1, :] < SSD_HEADDIM, etot[:, ja:ja + 1], etot[:, ja + 1:ja + 2])
            new_state.append(state[q] * e_row + new)
        y_s[rows, :] += jnp.concatenate(y_pairs, axis=1)
        return tuple(new_state)

    def init_state(d):
        if has_init:
            s0 = init_ref[0, d].reshape(GROUP_W, D_STATE).T
        else:
            s0 = jnp.zeros((D_STATE, GROUP_W), F32)
        return tuple(s0[:, q * LANES:(q + 1) * LANES] for q in range(n_pairs))

    def step(i, carry):
        return (scan_chunk(0, i, carry[0]), scan_chunk(1, n_chunks - 1 - i, carry[1]))

    final = lax.fori_loop(0, n_chunks, step, (init_state(0), init_state(1)))
    for d in range(2):
        fin_ref[0, d] = jnp.concatenate(final[d], axis=1).T.reshape(
            HEADS_PER_GROUP, SSD_HEADDIM, D_STATE)
    y_ref[...] = y_s[...].astype(y_ref.dtype)


def _ssd(xbc, conv_w, conv_b, dt_rows, dt_cols, bias_r, bias_c, a_r, a_c, dsk, init,
         *, n_batch, seq_len, row0):
    sb0 = row0 // seq_len
    n_chunks = seq_len // CHUNK
    n_dh = 2 * HEADS_PER_GROUP
    b_blk0 = D_INNER // D_STATE
    c_blk0 = (D_INNER + SSD_GROUPS * D_STATE) // D_STATE
    has_init = init is not None
    in_specs = [
        pl.BlockSpec((seq_len, GROUP_W), lambda b, g: (sb0 + b, g)),
        pl.BlockSpec((seq_len, D_STATE), lambda b, g: (sb0 + b, b_blk0 + g)),
        pl.BlockSpec((seq_len, D_STATE), lambda b, g: (sb0 + b, c_blk0 + g)),
        pl.BlockSpec((CONV_W, GROUP_W), lambda b, g: (0, g)),
        pl.BlockSpec((CONV_W, D_STATE), lambda b, g: (0, b_blk0 + g)),
        pl.BlockSpec((CONV_W, D_STATE), lambda b, g: (0, c_blk0 + g)),
        pl.BlockSpec((1, GROUP_W), lambda b, g: (0, g)),
        pl.BlockSpec((1, D_STATE), lambda b, g: (0, b_blk0 + g)),
        pl.BlockSpec((1, D_STATE), lambda b, g: (0, c_blk0 + g)),
        pl.BlockSpec((1, seq_len, n_dh), lambda b, g: (g, sb0 + b, 0)),
        pl.BlockSpec((1, n_chunks, n_dh, CHUNK), lambda b, g: (g, sb0 + b, 0, 0)),
        pl.BlockSpec((1, 1, n_dh), lambda b, g: (g, 0, 0)),
        pl.BlockSpec((1, n_dh, 1), lambda b, g: (g, 0, 0)),
        pl.BlockSpec((1, 1, n_dh), lambda b, g: (g, 0, 0)),
        pl.BlockSpec((1, n_dh, 1), lambda b, g: (g, 0, 0)),
        pl.BlockSpec((1, GROUP_W), lambda b, g: (0, g)),
    ]
    args = [xbc, xbc, xbc, conv_w, conv_w, conv_w, conv_b, conv_b, conv_b,
            dt_rows, dt_cols, bias_r, bias_c, a_r, a_c, dsk]
    state_spec = pl.BlockSpec((1, 2, HEADS_PER_GROUP, SSD_HEADDIM, D_STATE),
                              lambda b, g: (b, 0, g, 0, 0))
    if has_init:
        in_specs.append(state_spec)
        args.append(init)
    padded = seq_len + 2 * CONV_HALO
    return pl.pallas_call(
        functools.partial(_ssd_kernel, seq_len=seq_len, has_init=has_init),
        grid=(n_batch, SSD_GROUPS),
        in_specs=in_specs,
        out_specs=[pl.BlockSpec((seq_len, GROUP_W), lambda b, g: (b, g)), state_spec],
        out_shape=[jax.ShapeDtypeStruct((n_batch * seq_len, D_INNER), BF16),
                   jax.ShapeDtypeStruct((n_batch, 2, SSD_HEADS, SSD_HEADDIM, D_STATE), F32)],
        scratch_shapes=[pltpu.VMEM((padded, GROUP_W), F32),
                        pltpu.VMEM((padded, D_STATE), F32),
                        pltpu.VMEM((padded, D_STATE), F32),
                        pltpu.VMEM((n_chunks, HEADS_PER_GROUP // 2, 2 * CHUNK, LANES), BF16),
                        pltpu.VMEM((n_chunks, D_STATE, CHUNK), F32),
                        pltpu.VMEM((n_chunks, CHUNK, D_STATE), BF16),
                        pltpu.VMEM((n_chunks, CHUNK, D_STATE), F32),
                        pltpu.VMEM((seq_len, n_dh), F32),
                        pltpu.VMEM((n_chunks, n_dh, CHUNK), F32),
                        pltpu.VMEM((seq_len, GROUP_W), F32)],
        compiler_params=_params("arbitrary", "arbitrary"),
        name="ssd",
    )(*args)


def _ossd_kernel(yc_ref, yl_ref, z_ref, ng_ref, w_ref, g_ref, om_ref, o_ref, a_s, *, n_ctx_tiles):
    i = pl.program_id(0)

    def normalise(y_ref):
        for r0 in range(0, a_s.shape[0], CHUNK):
            rows = slice(r0, r0 + CHUNK)
            u = y_ref[rows, :].astype(F32) * _silu(z_ref[rows, :].astype(F32))
            a_s[rows, :] = (_rms(u) * ng_ref[...]).astype(a_s.dtype)

    first = pl.program_id(1) == 0
    pl.when(first & (i < n_ctx_tiles))(lambda: normalise(yc_ref))
    pl.when(first & (i >= n_ctx_tiles))(lambda: normalise(yl_ref))

    p = jnp.dot(a_s[...], w_ref[...], preferred_element_type=F32)
    o_ref[...] = (p * g_ref[...].astype(F32) + om_ref[...]).astype(o_ref.dtype)


def _ossd(y_ctx, y_lat, z, norm_g, w, gates, omla):
    tm, tn = 512, 512
    n_ctx_tiles = N_CTX // tm
    g_blk0 = D_MODEL // tn
    return pl.pallas_call(
        functools.partial(_ossd_kernel, n_ctx_tiles=n_ctx_tiles),
        grid=(N_TOK // tm, D_MODEL // tn),
        in_specs=[pl.BlockSpec((tm, D_INNER), lambda i, j: (jnp.minimum(i, n_ctx_tiles - 1), 0)),
                  pl.BlockSpec((tm, D_INNER), lambda i, j: (jnp.maximum(i - n_ctx_tiles, 0), 0)),
                  pl.BlockSpec((tm, D_INNER), lambda i, j: (i, 0)),
                  pl.BlockSpec((1, D_INNER), lambda i, j: (0, 0)),
                  pl.BlockSpec((D_INNER, tn), lambda i, j: (0, j)),
                  pl.BlockSpec((tm, tn), lambda i, j: (i, g_blk0 + j)),
                  pl.BlockSpec((tm, tn), lambda i, j: (i, j))],
        out_specs=pl.BlockSpec((tm, tn), lambda i, j: (i, j)),
        out_shape=jax.ShapeDtypeStruct((N_TOK, D_MODEL), BF16),
        scratch_shapes=[pltpu.VMEM((tm, D_INNER), BF16)],
        compiler_params=_params("arbitrary", "arbitrary"),
        name="ossd_merge",
    )(y_ctx, y_lat, z, norm_g.reshape(1, D_INNER), w, gates, omla)


def _pack_halves(v):
    n = v.shape[1] // 2
    lo = lax.bitcast_convert_type(v[:, :n].astype(BF16).astype(F32), jnp.uint32)
    hi = lax.bitcast_convert_type(v[:, n:].astype(BF16).astype(F32), jnp.uint32)
    return (lo >> 16) | (hi & jnp.uint32(0xFFFF0000))


def _unpack_halves(w):
    lo = lax.bitcast_convert_type(w << 16, F32).astype(BF16)
    hi = lax.bitcast_convert_type(w & jnp.uint32(0xFFFF0000), F32).astype(BF16)
    return lo, hi


def _outproj_kernel(a_ref, w_ref, x_ref, pg_ref, g1_ref, p2_ref, sc_ref, sh_ref, wr_ref,
                    x1_ref, h2_ref, lg_ref):
    out = jnp.dot(a_ref[...], w_ref[...], preferred_element_type=F32)
    x1 = x_ref[...] + g1_ref[0] * (_rms(out) * pg_ref[...])
    x1_ref[...] = x1
    h2 = (_rms(x1) * p2_ref[...]) * (1.0 + sc_ref[0]) + sh_ref[0]
    h2_ref[...] = _pack_halves(h2)
    lg_ref[...] = jnp.dot(h2.astype(BF16), wr_ref[...], preferred_element_type=F32)


def _outproj(merged, w_out, x, post1_g, pre2_g, mod, w_router_pad):
    tm = 256
    r = _mod_row(tm)
    row = lambda i: (i, 0)
    one = lambda i: (0, 0)
    return pl.pallas_call(
        _outproj_kernel,
        grid=(N_TOK // tm,),
        in_specs=[pl.BlockSpec((tm, D_MODEL), row),
                  pl.BlockSpec((D_MODEL, D_MODEL), one),
                  pl.BlockSpec((tm, D_MODEL), row),
                  pl.BlockSpec((1, D_MODEL), one),
                  pl.BlockSpec((1, 1, D_MODEL), lambda i: (r(i), 0, 2)),
                  pl.BlockSpec((1, D_MODEL), one),
                  pl.BlockSpec((1, 1, D_MODEL), lambda i: (r(i), 0, 4)),
                  pl.BlockSpec((1, 1, D_MODEL), lambda i: (r(i), 0, 3)),
                  pl.BlockSpec((D_MODEL, LANES), one)],
        out_specs=[pl.BlockSpec((tm, D_MODEL), row),
                   pl.BlockSpec((tm, D_MODEL // 2), row),
                   pl.BlockSpec((tm, LANES), row)],
        out_shape=[jax.ShapeDtypeStruct((N_TOK, D_MODEL), F32),
                   jax.ShapeDtypeStruct((N_TOK, D_MODEL // 2), jnp.uint32),
                   jax.ShapeDtypeStruct((N_TOK, LANES), F32)],
        compiler_params=_params("arbitrary"),
        name="outproj",
    )(merged, w_out, x, post1_g.reshape(1, D_MODEL), mod, pre2_g.reshape(1, D_MODEL), mod, mod,
      w_router_pad)


GATHER_STEP = 1


def _ffn_kernel(idx_ref, pos_ref, h2_hbm, wg_ref, wu_ref, wd_ref, gv_ref, ys_hbm,
                xbuf, acc_ref, gsem, ssem):
    e = pl.program_id(0)
    f = pl.program_id(1)
    n_e = pl.num_programs(0)
    n_f = pl.num_programs(1)
    n_rows = xbuf.shape[1]
    slot = lax.rem(e, 2)

    def gather_rows(expert, buf):
        def body(i, carry):
            pltpu.make_async_copy(h2_hbm.at[pl.ds(idx_ref[expert * n_rows + i], 1), :],
                                  xbuf.at[buf, pl.ds(i, 1), :], gsem.at[buf]).start()
            return carry
        lax.fori_loop(0, n_rows, body, 0, unroll=8)

    def scatter_rows(expert, buf):
        def body(i, carry):
            pltpu.make_async_copy(xbuf.at[buf, pl.ds(i, 1), :],
                                  ys_hbm.at[pl.ds(pos_ref[expert * n_rows + i], 1), :],
                                  ssem.at[buf]).start()
            return carry
        lax.fori_loop(0, n_rows, body, 0, unroll=8)

    def wait_gather(buf):
        pltpu.make_async_copy(h2_hbm.at[pl.ds(0, n_rows), :], xbuf.at[buf], gsem.at[buf]).wait()

    def wait_scatter(buf):
        pltpu.make_async_copy(xbuf.at[buf], ys_hbm.at[pl.ds(0, n_rows), :], ssem.at[buf]).wait()

    @pl.when((e == 0) & (f == 0))
    def _():
        gather_rows(0, 0)

    @pl.when(f == 0)
    def _():
        wait_gather(slot)

    wg = wg_ref[0].astype(BF16)
    wu = wu_ref[0].astype(BF16)
    wd = wd_ref[0].astype(BF16)
    for r0 in range(0, n_rows, FFN_ROWS):
        rows = slice(r0, r0 + FFN_ROWS)
        x = jnp.concatenate(_unpack_halves(xbuf[slot, rows, :]), axis=1)
        g = jnp.dot(x, wg, preferred_element_type=F32)
        u = jnp.dot(x, wu, preferred_element_type=F32)
        hid = (_silu(g) * u).astype(BF16)
        part = jnp.dot(hid, wd, preferred_element_type=F32)

        @pl.when(f == 0)
        def _():
            acc_ref[rows, :] = part

        @pl.when(f > 0)
        def _():
            acc_ref[rows, :] += part

        @pl.when(f == n_f - 1)
        def _():
            xbuf[slot, rows, :] = _pack_halves(acc_ref[rows, :] * gv_ref[0, rows, :])

    @pl.when(f == n_f - 1)
    def _():
        scatter_rows(e, slot)

    @pl.when((f == GATHER_STEP) & (e + 1 < n_e))
    def _():
        @pl.when(e >= 1)
        def _():
            wait_scatter(1 - slot)
        gather_rows(e + 1, 1 - slot)

    @pl.when((e == n_e - 1) & (f == n_f - 1))
    def _():
        wait_scatter(1 - slot)
        wait_scatter(slot)


def _ffn(idx, pos, h2p, w_gate, w_up, w_down, gval):
    n_rows = gval.shape[1]
    tf = 256
    n_f = D_EXPERT // tf
    assert N_EXPERTS >= 2 and n_f > GATHER_STEP
    return pl.pallas_call(
        _ffn_kernel,
        grid_spec=pltpu.PrefetchScalarGridSpec(
            num_scalar_prefetch=2,
            grid=(N_EXPERTS, n_f),
            in_specs=[pl.BlockSpec(memory_space=pl.ANY),
                      pl.BlockSpec((1, D_MODEL, tf), lambda e, f, i, p: (e, 0, f)),
                      pl.BlockSpec((1, D_MODEL, tf), lambda e, f, i, p: (e, 0, f)),
                      pl.BlockSpec((1, tf, D_MODEL), lambda e, f, i, p: (e, f, 0)),
                      pl.BlockSpec((1, n_rows, 1), lambda e, f, i, p: (e, 0, 0))],
            out_specs=pl.BlockSpec(memory_space=pl.ANY),
            scratch_shapes=[pltpu.VMEM((2, n_rows, D_MODEL // 2), jnp.uint32),
                            pltpu.VMEM((n_rows, D_MODEL), F32),
                            pltpu.SemaphoreType.DMA((2,)),
                            pltpu.SemaphoreType.DMA((2,))]),
        out_shape=jax.ShapeDtypeStruct((N_EXPERTS * n_rows, D_MODEL // 2), jnp.uint32),
        compiler_params=_params("arbitrary", "arbitrary"),
        name="expert_ffn",
    )(idx, pos, h2p, w_gate, w_up, w_down, gval)


COMBINE_TILE = 256


def _combine_kernel(wt_ref, wb_ref, wfirst_ref, wlast_ref, wvalid_ref,
                    ys_ref, s0_ref, s1_ref, x1_ref, pg_ref, g2_ref, o_ref, acc_ref):
    w = pl.program_id(0)
    half = acc_ref.shape[1] // 2

    @pl.when(wfirst_ref[w] == 1)
    def _():
        acc_ref[...] = jnp.zeros_like(acc_ref)

    @pl.when(wvalid_ref[w] == 1)
    def _():
        row = wb_ref[w] * COMBINE_TILE + lax.broadcasted_iota(
            jnp.int32, (COMBINE_TILE, COMBINE_TILE), 1)
        own = (row >= s0_ref[...]) & (row < s1_ref[...])
        sel = jnp.where(own, 1.0, 0.0).astype(BF16)
        lo, hi = _unpack_halves(ys_ref[...])
        acc_ref[:, :half] += jnp.dot(sel, lo, preferred_element_type=F32)
        acc_ref[:, half:] += jnp.dot(sel, hi, preferred_element_type=F32)

    @pl.when(wlast_ref[w] == 1)
    def _():
        o_ref[...] = x1_ref[...] + g2_ref[0] * (_rms(acc_ref[...]) * pg_ref[...])


def _combine(plan, ys, seg0, seg1, x1, post2_g, mod):
    n_tok = x1.shape[0]
    tile = COMBINE_TILE
    r = _mod_row(tile)
    n_work = plan[0].shape[0]
    tok = lambda w, wt, wb, wf, wl, wv: (wt[w], 0)
    return pl.pallas_call(
        _combine_kernel,
        grid_spec=pltpu.PrefetchScalarGridSpec(
            num_scalar_prefetch=5,
            grid=(n_work,),
            in_specs=[pl.BlockSpec((tile, D_MODEL // 2), lambda w, wt, wb, wf, wl, wv: (wb[w], 0)),
                      pl.BlockSpec((tile, 1), tok),
                      pl.BlockSpec((tile, 1), tok),
                      pl.BlockSpec((tile, D_MODEL), tok),
                      pl.BlockSpec((1, D_MODEL), lambda w, wt, wb, wf, wl, wv: (0, 0)),
                      pl.BlockSpec((1, 1, D_MODEL), lambda w, wt, wb, wf, wl, wv: (r(wt[w]), 0, 5))],
            out_specs=pl.BlockSpec((tile, D_MODEL), tok),
            scratch_shapes=[pltpu.VMEM((tile, D_MODEL), F32)]),
        out_shape=jax.ShapeDtypeStruct((n_tok, D_MODEL), F32),
        compiler_params=_params("arbitrary"),
        name="combine",
    )(*plan, ys, seg0, seg1, x1, post2_g.reshape(1, D_MODEL), mod)


def _moe_plan(idx, n_tok):
    n_e, n_r = idx.shape
    tile = COMBINE_TILE
    chosen = jnp.zeros((n_e, n_tok), jnp.int32).at[jnp.arange(n_e)[:, None], idx].set(1)
    seg = jnp.concatenate([jnp.zeros((1,), jnp.int32), jnp.cumsum(chosen.sum(0))])
    before = jnp.cumsum(chosen, axis=0) - chosen
    pos = seg[idx] + jnp.take_along_axis(before, idx, axis=1)

    n_tiles = n_tok // tile
    n_blocks = n_e * n_r // tile
    off = seg[::tile]
    first_blk = jnp.minimum(off[:-1] // tile, n_blocks - 1)
    last_blk = jnp.maximum((off[1:] - 1) // tile, first_blk)
    n_blk = last_blk - first_blk + 1
    start = jnp.cumsum(n_blk) - n_blk
    total = n_blk.sum()
    w = jnp.arange(n_tiles + n_blocks, dtype=jnp.int32)
    valid = w < total
    wt = jnp.clip(jnp.searchsorted(start, w, side="right") - 1, 0, n_tiles - 1)
    wt = jnp.where(valid, wt, n_tiles - 1).astype(jnp.int32)
    wb = jnp.where(valid, first_blk[wt] + (w - start[wt]), n_blocks - 1).astype(jnp.int32)
    first = (valid & (w == start[wt])).astype(jnp.int32)
    last = (valid & (w == start[wt] + n_blk[wt] - 1)).astype(jnp.int32)
    return pos.astype(jnp.int32), seg, (wt, wb, first, last, valid.astype(jnp.int32))


def _rope_tables():
    rows = DEC_SEQ // GRID_W
    row = jnp.repeat(jnp.arange(rows), GRID_W).astype(F32)
    col = jnp.tile(jnp.arange(GRID_W), rows).astype(F32)
    half = QK_ROPE // 2
    inv = ROPE_THETA ** (-jnp.arange(0, half, 2, dtype=F32) / half)
    ang_r = row[:, None] * inv[None, :]
    ang_c = col[:, None] * inv[None, :]
    ang = jnp.concatenate([ang_r, ang_r, ang_c, ang_c], axis=-1)
    cos, sin = jnp.cos(ang), jnp.sin(ang)
    ident = 256
    one = lambda n: jnp.ones((DEC_SEQ, n), F32)
    zero = lambda n: jnp.zeros((DEC_SEQ, n), F32)

    def table(parts_cos, parts_sin, width):
        c = jnp.concatenate([jnp.ones((ident, width), F32), jnp.concatenate(parts_cos, -1)], 0)
        s = jnp.concatenate([jnp.zeros((ident, width), F32), jnp.concatenate(parts_sin, -1)], 0)
        return c, s

    cos_k, sin_k = table([cos, one(LANES - QK_ROPE)], [sin, zero(LANES - QK_ROPE)], LANES)
    pad = Q_HEAD_PAD - QK_NOPE - QK_ROPE
    cos_q, sin_q = table([one(QK_NOPE), cos, one(pad)], [zero(QK_NOPE), sin, zero(pad)], Q_HEAD_PAD)
    q_scale = (QK_NOPE + QK_ROPE) ** -0.5 * np.log2(np.e)
    return cos_k, sin_k, cos_q * q_scale, sin_q * q_scale


def _route(logits, n_tok):
    cap = CAP_FACTOR * n_tok // N_EXPERTS
    aff = jax.nn.softmax(logits, axis=-1)
    return lax.top_k(aff.T, cap)


def kernel(x_prompt, x_sample, cache_ckv, cache_krope, state_ssm, c, c_ctx, w_mod, b_mod,
           pre1_g, post1_g, pre2_g, post2_g, w_in, q_norm_g, w_q_b, kv_norm_g, w_kv_b, w_o_mla,
           conv_w, conv_b, dt_bias, a_log, d_skip, ssd_norm_g, w_o_ssd, w_out, w_router,
           w_e_gate, w_e_up, w_e_down):
    l = 0
    x = jnp.concatenate([x_prompt.reshape(N_CTX, D_MODEL), x_sample.reshape(N_LAT, D_MODEL)], 0)

    n_mod = 1 + DEC_BATCH
    cvec = jnp.concatenate([c_ctx[None, :], c, jnp.zeros((16 - n_mod, D_MODEL), F32)], 0)
    mod = _adaln(cvec, w_mod[l], b_mod[l]).reshape(16, 1, 6 * D_MODEL)

    wi = w_in[l]
    o_z = Q_RANK + KV_RANK + QK_ROPE
    o_xbc = o_z + D_INNER
    o_dt = o_xbc + XBC_DIM
    o_gate = o_dt + 2 * SSD_HEADS
    w_small = jnp.concatenate([wi[:, :o_z], jnp.zeros((D_MODEL, LANES - QK_ROPE), F32),
                               wi[:, o_dt:o_gate]], axis=1).astype(BF16)
    w_z = wi[:, o_z:o_xbc].astype(BF16)
    w_xbc = wi[:, o_xbc:o_dt].astype(BF16)
    w_gates = wi[:, o_gate:].astype(BF16)
    wq = w_q_b[l].reshape(Q_RANK, N_HEADS, QK_NOPE + QK_ROPE)
    wq = jnp.concatenate([wq, jnp.zeros((Q_RANK, N_HEADS, Q_HEAD_PAD - QK_NOPE - QK_ROPE), F32)],
                         axis=-1).reshape(Q_RANK, N_HEADS * Q_HEAD_PAD).astype(BF16)
    cos_k, sin_k, cos_q, sin_q = _rope_tables()

    h1 = _prenorm(x, pre1_g[l], mod, 1, 0)
    cqn, ckv, kr, dt_raw = _inproj_small(h1, w_small, q_norm_g[l], kv_norm_g[l], cos_k, sin_k)
    z = _mm(h1, w_z, tm=512, tn=1024, out_dtype=BF16, name="inproj_z")
    xbc = _mm(h1, w_xbc, tm=512, tn=1024, out_dtype=BF16, name="inproj_xbc")
    gates = _mm(h1, w_gates, tm=512, tn=1024, out_dtype=BF16, act="sigmoid", name="inproj_gates")

    q = _qproj(cqn, wq, cos_q, sin_q)
    ckv_b = ckv.astype(BF16)
    kr_b = kr.astype(BF16)
    keys_ctx = ckv_b[:N_CTX]
    keys_lat = jnp.concatenate([cache_ckv[:, l].astype(BF16),
                                ckv_b[N_CTX:].reshape(DEC_BATCH, DEC_SEQ, KV_RANK)], axis=1)
    keys_lat = keys_lat.reshape(DEC_BATCH * (PAST_LEN + DEC_SEQ), KV_RANK)
    kr_cache = jnp.concatenate([cache_krope[:, l].astype(BF16),
                                jnp.zeros((DEC_BATCH, PAST_LEN, LANES - QK_ROPE), BF16)], axis=-1)
    kr_lat = jnp.concatenate([kr_cache, kr_b[N_CTX:].reshape(DEC_BATCH, DEC_SEQ, LANES)], axis=1)
    kr_lat = kr_lat.reshape(DEC_BATCH * (PAST_LEN + DEC_SEQ), LANES)
    wkv = w_kv_b[l].astype(BF16)
    kv_ctx = _mm(keys_ctx, wkv, tm=512, tn=2048, out_dtype=BF16, name="kv_ctx")
    kv_lat = _mm(keys_lat, wkv, tm=512, tn=2048, out_dtype=BF16, name="kv_lat")
    o_ctx = _attention(q, kv_ctx, kr_b[:N_CTX], n_batch=BATCH, lq=SEQ, lk=SEQ, tq=SEQ,
                       heads=N_HEADS, q_row0=0)
    o_lat = _attention(q, kv_lat, kr_lat, n_batch=DEC_BATCH, lq=DEC_SEQ, lk=PAST_LEN + DEC_SEQ,
                       tq=512, heads=1, q_row0=N_CTX)
    omla = _omla(o_ctx, o_lat, w_o_mla[l].astype(BF16), gates)

    n_dh = 2 * HEADS_PER_GROUP
    dt4 = dt_raw.reshape(N_TOK, 2, SSD_GROUPS, HEADS_PER_GROUP)
    dt_rows = dt4.transpose(2, 0, 1, 3).reshape(SSD_GROUPS, N_TOK, n_dh)
    dt_cols = dt_rows.reshape(SSD_GROUPS, N_TOK // CHUNK, CHUNK, n_dh).transpose(0, 1, 3, 2)
    per_group = lambda v: v.reshape(2, SSD_GROUPS, HEADS_PER_GROUP).transpose(1, 0, 2).reshape(
        SSD_GROUPS, n_dh)
    bias_g = per_group(dt_bias[l])
    a_g = per_group(-jnp.exp(a_log[l]))
    dsk = jnp.repeat(d_skip[l], SSD_HEADDIM).reshape(1, D_INNER)
    ssd_args = (xbc, conv_w[l], conv_b[l].reshape(1, XBC_DIM), dt_rows, dt_cols,
                bias_g[:, None, :], bias_g[:, :, None], a_g[:, None, :], a_g[:, :, None], dsk)
    y_ctx, ssm_ctx = _ssd(*ssd_args, None, n_batch=BATCH, seq_len=SEQ, row0=0)
    y_lat, _ = _ssd(*ssd_args, state_ssm[:, l], n_batch=DEC_BATCH, seq_len=DEC_SEQ, row0=N_CTX)
    merged = _ossd(y_ctx, y_lat, z, ssd_norm_g[l], w_o_ssd[l].astype(BF16), gates, omla)

    w_router_pad = jnp.concatenate([w_router[l], jnp.zeros((D_MODEL, LANES - N_EXPERTS), F32)],
                                   axis=1).astype(BF16)
    x1, h2p, logits = _outproj(merged, w_out[l].astype(BF16), x, post1_g[l], pre2_g[l], mod,
                               w_router_pad)

    logits = logits[:, :N_EXPERTS]
    gv_c, idx_c = _route(logits[:N_CTX], N_CTX)
    gv_l, idx_l = _route(logits[N_CTX:], N_LAT)
    idx = jnp.concatenate([idx_c, idx_l + N_CTX], axis=1)
    gval = jnp.concatenate([gv_c, gv_l], axis=1)
    pos, seg, plan = _moe_plan(idx, N_TOK)
    ys = _ffn(idx.reshape(-1), pos.reshape(-1), h2p, w_e_gate[l], w_e_up[l], w_e_down[l],
              gval[..., None])
    out = _combine(plan, ys, seg[:-1].reshape(N_TOK, 1), seg[1:].reshape(N_TOK, 1), x1,
                   post2_g[l], mod)

    y_p = out[:N_CTX].reshape(BATCH, SEQ, D_MODEL)
    y_s = out[N_CTX:].reshape(DEC_BATCH, DEC_SEQ, D_MODEL)
    new_ckv = ckv[:N_CTX].reshape(BATCH, 1, SEQ, KV_RANK)
    new_krope = kr[:N_CTX, :QK_ROPE].reshape(BATCH, 1, SEQ, QK_ROPE)
    new_ssm = ssm_ctx[:, None]
    return (y_p, y_s, new_ckv, new_krope, new_ssm)
```

```python
import functools

import jax
import jax.numpy as jnp
import numpy as np
from jax import lax
from jax.experimental import pallas as pl
from jax.experimental.pallas import tpu as pltpu

F32 = jnp.float32
BF16 = jnp.bfloat16

D_MODEL = 2048
BATCH = 16
SEQ = 256
DEC_BATCH = 8
DEC_SEQ = 1024
PAST_LEN = 512
GRID_W = 64
N_HEADS = 16
Q_RANK = 512
KV_RANK = 512
QK_NOPE = 128
QK_ROPE = 64
V_DIM = 128
ROPE_THETA = 10000.0
D_INNER = 2 * D_MODEL
SSD_HEADDIM = 64
SSD_HEADS = D_INNER // SSD_HEADDIM
SSD_GROUPS = 8
HEADS_PER_GROUP = SSD_HEADS // SSD_GROUPS
D_STATE = 128
CONV_W = 5
CHUNK = 128
XBC_DIM = D_INNER + 2 * SSD_GROUPS * D_STATE
N_EXPERTS = 16
CAP_FACTOR = 2
D_EXPERT = 1536
EPS = 1e-6

N_CTX = BATCH * SEQ
N_LAT = DEC_BATCH * DEC_SEQ
N_TOK = N_CTX + N_LAT
Q_HEAD_PAD = 256
GROUP_W = HEADS_PER_GROUP * SSD_HEADDIM
LANES = 128
VMEM_LIMIT = 56 * 1024 * 1024
CONV_PAD = 16
FFN_ROWS = 512


def _params(*sem):
    return pltpu.CompilerParams(dimension_semantics=sem, vmem_limit_bytes=VMEM_LIMIT)


def _rms(x):
    return x * lax.rsqrt(jnp.mean(x * x, axis=-1, keepdims=True) + EPS)


def _silu(x):
    return x * jax.nn.sigmoid(x)


def _mod_row(tm):
    n_ctx_tiles = N_CTX // tm
    per_lat = DEC_SEQ // tm
    return lambda i: jnp.where(i < n_ctx_tiles, 0, 1 + (i - n_ctx_tiles) // per_lat)


def _rope_row(tm):
    n_ctx_tiles = N_CTX // tm
    per_lat = DEC_SEQ // tm
    return lambda i: jnp.where(i < n_ctx_tiles, 0, 1 + (i - n_ctx_tiles) % per_lat)


def _rot_half(x):
    w = x.shape[1]
    nxt = pltpu.roll(x, w - 16, axis=1)
    prv = pltpu.roll(x, 16, axis=1)
    lane = lax.broadcasted_iota(jnp.int32, x.shape, 1)
    return jnp.where((lane % 32) < 16, -nxt, prv)


def _adaln_kernel(c_ref, w_ref, b_ref, o_ref):
    a = _silu(c_ref[...]).astype(BF16)
    o_ref[...] = jnp.dot(a, w_ref[...].astype(BF16), preferred_element_type=F32) + b_ref[...]


def _adaln(cvec, w_mod, b_mod):
    rows = cvec.shape[0]
    n = w_mod.shape[1]
    tn = 1024
    return pl.pallas_call(
        _adaln_kernel,
        grid=(n // tn,),
        in_specs=[pl.BlockSpec((rows, D_MODEL), lambda j: (0, 0)),
                  pl.BlockSpec((D_MODEL, tn), lambda j: (0, j)),
                  pl.BlockSpec((1, tn), lambda j: (0, j))],
        out_specs=pl.BlockSpec((rows, tn), lambda j: (0, j)),
        out_shape=jax.ShapeDtypeStruct((rows, n), F32),
        compiler_params=_params("arbitrary"),
        name="adaln",
    )(cvec, w_mod, b_mod.reshape(1, n))


def _ctx_lat_specs(tm, width):
    n_ctx_tiles = N_CTX // tm
    ctx = pl.BlockSpec((tm, width), lambda i: (jnp.minimum(i, n_ctx_tiles - 1), 0))
    lat = pl.BlockSpec((tm, width), lambda i: (jnp.maximum(i - n_ctx_tiles, 0), 0))
    return ctx, lat


def _prenorm_kernel(xc_ref, xl_ref, g_ref, sc_ref, sh_ref, o_ref, *, n_ctx_tiles):
    def run(x_ref):
        y = _rms(x_ref[...]) * g_ref[...]
        o_ref[...] = (y * (1.0 + sc_ref[0]) + sh_ref[0]).astype(o_ref.dtype)

    i = pl.program_id(0)
    pl.when(i < n_ctx_tiles)(lambda: run(xc_ref))
    pl.when(i >= n_ctx_tiles)(lambda: run(xl_ref))


def _prenorm(x_ctx, x_lat, gain, mod, sc_blk, sh_blk):
    tm = 256
    r = _mod_row(tm)
    ctx, lat = _ctx_lat_specs(tm, D_MODEL)
    return pl.pallas_call(
        functools.partial(_prenorm_kernel, n_ctx_tiles=N_CTX // tm),
        grid=(N_TOK // tm,),
        in_specs=[ctx, lat,
                  pl.BlockSpec((1, D_MODEL), lambda i: (0, 0)),
                  pl.BlockSpec((1, 1, D_MODEL), lambda i: (r(i), 0, sc_blk)),
                  pl.BlockSpec((1, 1, D_MODEL), lambda i: (r(i), 0, sh_blk))],
        out_specs=pl.BlockSpec((tm, D_MODEL), lambda i: (i, 0)),
        out_shape=jax.ShapeDtypeStruct((N_TOK, D_MODEL), BF16),
        compiler_params=_params("arbitrary"),
        name="prenorm",
    )(x_ctx, x_lat, gain.reshape(1, D_MODEL), mod, mod)


def _inproj_small_kernel(h_ref, w_ref, qg_ref, kvg_ref, cos_ref, sin_ref,
                         cqn_ref, ckv_ref, kr_ref, dt_ref):
    p = jnp.dot(h_ref[...], w_ref[...], preferred_element_type=F32)
    cqn_ref[...] = (_rms(p[:, :Q_RANK]) * qg_ref[...]).astype(cqn_ref.dtype)
    ckv_ref[...] = _rms(p[:, Q_RANK:Q_RANK + KV_RANK]) * kvg_ref[...]
    kr = p[:, Q_RANK + KV_RANK:Q_RANK + KV_RANK + LANES]
    kr_ref[...] = kr * cos_ref[...] + _rot_half(kr) * sin_ref[...]
    dt_ref[...] = p[:, Q_RANK + KV_RANK + LANES:]


def _inproj_small(h, w, q_norm_g, kv_norm_g, cos_k, sin_k):
    tm = 256
    n = w.shape[1]
    rr = _rope_row(tm)
    row = lambda i: (i, 0)
    return pl.pallas_call(
        _inproj_small_kernel,
        grid=(N_TOK // tm,),
        in_specs=[pl.BlockSpec((tm, D_MODEL), row),
                  pl.BlockSpec((D_MODEL, n), lambda i: (0, 0)),
                  pl.BlockSpec((1, Q_RANK), lambda i: (0, 0)),
                  pl.BlockSpec((1, KV_RANK), lambda i: (0, 0)),
                  pl.BlockSpec((tm, LANES), lambda i: (rr(i), 0)),
                  pl.BlockSpec((tm, LANES), lambda i: (rr(i), 0))],
        out_specs=[pl.BlockSpec((tm, Q_RANK), row),
                   pl.BlockSpec((tm, KV_RANK), row),
                   pl.BlockSpec((tm, LANES), row),
                   pl.BlockSpec((tm, LANES), row)],
        out_shape=[jax.ShapeDtypeStruct((N_TOK, Q_RANK), BF16),
                   jax.ShapeDtypeStruct((N_TOK, KV_RANK), F32),
                   jax.ShapeDtypeStruct((N_TOK, LANES), F32),
                   jax.ShapeDtypeStruct((N_TOK, LANES), F32)],
        compiler_params=_params("arbitrary"),
        name="inproj_small",
    )(h, w, q_norm_g.reshape(1, Q_RANK), kv_norm_g.reshape(1, KV_RANK), cos_k, sin_k)


def _mm_kernel(a_ref, w_ref, o_ref, *, act):
    p = jnp.dot(a_ref[...], w_ref[...], preferred_element_type=F32)
    if act == "sigmoid":
        p = jax.nn.sigmoid(p)
    elif act == "silu":
        p = _silu(p)
    o_ref[...] = p.astype(o_ref.dtype)


def _mm(a, w, *, tm, tn, out_dtype, act=None, name="mm"):
    m, k = a.shape
    n = w.shape[1]
    return pl.pallas_call(
        functools.partial(_mm_kernel, act=act),
        grid=(n // tn, m // tm),
        in_specs=[pl.BlockSpec((tm, k), lambda j, i: (i, 0)),
                  pl.BlockSpec((k, tn), lambda j, i: (0, j))],
        out_specs=pl.BlockSpec((tm, tn), lambda j, i: (i, j)),
        out_shape=jax.ShapeDtypeStruct((m, n), out_dtype),
        compiler_params=_params("arbitrary", "arbitrary"),
        name=name,
    )(a, w)


def _mm_nt_kernel(a_ref, b_ref, o_ref):
    o_ref[...] = lax.dot_general(a_ref[...], b_ref[...], (((1,), (1,)), ((), ())),
                                 preferred_element_type=F32).astype(o_ref.dtype)


def _mm_nt(a, b, *, tn, out_dtype, name):
    m, k = a.shape
    n = b.shape[0]
    return pl.pallas_call(
        _mm_nt_kernel,
        grid=(n // tn,),
        in_specs=[pl.BlockSpec((m, k), lambda j: (0, 0)),
                  pl.BlockSpec((tn, k), lambda j: (j, 0))],
        out_specs=pl.BlockSpec((m, tn), lambda j: (0, j)),
        out_shape=jax.ShapeDtypeStruct((m, n), out_dtype),
        compiler_params=_params("arbitrary"),
        name=name,
    )(a, b)


def _qproj_kernel(a_ref, w_ref, cos_ref, sin_ref, o_ref):
    p = jnp.dot(a_ref[...], w_ref[...], preferred_element_type=F32)
    cos = cos_ref[...]
    sin = sin_ref[...]
    for h in range(N_HEADS):
        sl = slice(h * Q_HEAD_PAD, (h + 1) * Q_HEAD_PAD)
        ph = p[:, sl]
        o_ref[:, sl] = (ph * cos + _rot_half(ph) * sin).astype(o_ref.dtype)


def _qproj(cqn, wq, cos_q, sin_q):
    tm = 256
    n = wq.shape[1]
    rr = _rope_row(tm)
    return pl.pallas_call(
        _qproj_kernel,
        grid=(N_TOK // tm,),
        in_specs=[pl.BlockSpec((tm, Q_RANK), lambda i: (i, 0)),
                  pl.BlockSpec((Q_RANK, n), lambda i: (0, 0)),
                  pl.BlockSpec((tm, Q_HEAD_PAD), lambda i: (rr(i), 0)),
                  pl.BlockSpec((tm, Q_HEAD_PAD), lambda i: (rr(i), 0))],
        out_specs=pl.BlockSpec((tm, n), lambda i: (i, 0)),
        out_shape=jax.ShapeDtypeStruct((N_TOK, n), BF16),
        compiler_params=_params("arbitrary"),
        name="qproj",
    )(cqn, wq, cos_q, sin_q)


ATTN_Q_ROWS = 256


def _attn_kernel(q_ref, knt_ref, krt_ref, v_ref, o_ref, *, heads):
    kr_t = krt_ref[...]
    for h in range(heads):
        k_t = jnp.concatenate([knt_ref[h * QK_NOPE:(h + 1) * QK_NOPE, :], kr_t], axis=0)
        v = v_ref[:, h * V_DIM:(h + 1) * V_DIM]
        for r0 in range(0, q_ref.shape[0], ATTN_Q_ROWS):
            rows = slice(r0, r0 + ATTN_Q_ROWS)
            q = q_ref[rows, h * Q_HEAD_PAD:(h + 1) * Q_HEAD_PAD]
            s = jnp.dot(q, k_t, preferred_element_type=F32)
            p = jnp.exp2(s - jnp.max(s, axis=-1, keepdims=True))
            l = jnp.sum(p, axis=-1, keepdims=True)
            o = jnp.dot(p.astype(BF16), v, preferred_element_type=F32)
            o_ref[rows, h * V_DIM:(h + 1) * V_DIM] = (o / l).astype(o_ref.dtype)


def _attention(q, kn_t, kr_t, v, *, n_batch, lq, lk, tq, heads, q_row0):
    nq = lq // tq
    qb0 = q_row0 // tq
    return pl.pallas_call(
        functools.partial(_attn_kernel, heads=heads),
        grid=(n_batch, N_HEADS // heads, nq),
        in_specs=[pl.BlockSpec((tq, heads * Q_HEAD_PAD), lambda b, h, i: (qb0 + b * nq + i, h)),
                  pl.BlockSpec((heads * QK_NOPE, lk), lambda b, h, i: (h, b)),
                  pl.BlockSpec((LANES, lk), lambda b, h, i: (0, b)),
                  pl.BlockSpec((lk, heads * V_DIM), lambda b, h, i: (b, h))],
        out_specs=pl.BlockSpec((tq, heads * V_DIM), lambda b, h, i: (b * nq + i, h)),
        out_shape=jax.ShapeDtypeStruct((n_batch * lq, N_HEADS * V_DIM), BF16),
        compiler_params=_params("arbitrary", "arbitrary", "arbitrary"),
        name="attention",
    )(q, kn_t, kr_t, v)


def _split_rows_specs(tm, width, n_ctx_tiles):
    ctx = pl.BlockSpec((tm, width), lambda j, i: (jnp.minimum(i, n_ctx_tiles - 1), 0))
    lat = pl.BlockSpec((tm, width), lambda j, i: (jnp.maximum(i - n_ctx_tiles, 0), 0))
    return ctx, lat


def _omla_kernel(ac_ref, al_ref, w_ref, g_ref, o_ref, *, n_ctx_tiles):
    def run(a_ref):
        p = jnp.dot(a_ref[...], w_ref[...], preferred_element_type=F32)
        o_ref[...] = p * g_ref[...].astype(F32)

    i = pl.program_id(1)
    pl.when(i < n_ctx_tiles)(lambda: run(ac_ref))
    pl.when(i >= n_ctx_tiles)(lambda: run(al_ref))


def _omla(o_ctx, o_lat, w, gates):
    tm, tn = 512, 1024
    n_ctx_tiles = N_CTX // tm
    ctx, lat = _split_rows_specs(tm, D_MODEL, n_ctx_tiles)
    return pl.pallas_call(
        functools.partial(_omla_kernel, n_ctx_tiles=n_ctx_tiles),
        grid=(D_MODEL // tn, N_TOK // tm),
        in_specs=[ctx, lat,
                  pl.BlockSpec((D_MODEL, tn), lambda j, i: (0, j)),
                  pl.BlockSpec((tm, tn), lambda j, i: (i, j))],
        out_specs=pl.BlockSpec((tm, tn), lambda j, i: (i, j)),
        out_shape=jax.ShapeDtypeStruct((N_TOK, D_MODEL), F32),
        compiler_params=_params("arbitrary", "arbitrary"),
        name="omla",
    )(o_ctx, o_lat, w, gates)


def _softplus(x):
    return jnp.maximum(x, 0.0) + jnp.log1p(jnp.exp(-jnp.abs(x)))


def _conv_shift_mats():
    t = lax.broadcasted_iota(jnp.int32, (CHUNK, CHUNK + 2 * CONV_PAD), 0)
    j = lax.broadcasted_iota(jnp.int32, (CHUNK, CHUNK + 2 * CONV_PAD), 1)
    return [jnp.where(j == t + CONV_PAD + k - CONV_W // 2, 1.0, 0.0).astype(BF16) for k in range(CONV_W)]


def _conv_silu_chunk(src_ref, w_ref, b_ref, chunk, shift_mats):
    seq_len, width = src_ref.shape
    lo = chunk * CHUNK - CONV_PAD
    hi = (chunk + 1) * CHUNK + CONV_PAD
    parts = [src_ref[max(lo, 0):min(hi, seq_len), :]]
    if lo < 0:
        parts.insert(0, jnp.zeros((CONV_PAD, width), src_ref.dtype))
    if hi > seq_len:
        parts.append(jnp.zeros((CONV_PAD, width), src_ref.dtype))
    window = jnp.concatenate(parts, axis=0)
    mid = CONV_W // 2
    acc = b_ref[...] + w_ref[mid:mid + 1, :] * src_ref[chunk * CHUNK:(chunk + 1) * CHUNK, :].astype(F32)
    for k in range(CONV_W):
        if k != mid:
            acc = acc + w_ref[k:k + 1, :] * jnp.dot(shift_mats[k], window, preferred_element_type=F32)
    return _silu(acc)


def _head_pair_blockdiag(blk, lane):
    return jnp.concatenate([jnp.where(lane < SSD_HEADDIM, blk, 0.0),
                            jnp.where(lane >= SSD_HEADDIM, blk, 0.0)], axis=0)


def _ssd_kernel(*refs, seq_len, has_init):
    if has_init:
        (xs_ref, b_ref, c_ref, wx_ref, wb_ref, wc_ref, bx_ref, bb_ref, bc_ref,
         dtt_ref, bias_c_ref, a_c_ref, dsk_ref, init_ref,
         y_ref, fin_ref, rp_s, bt_s, cb_s, cbm_s, cst_s, gt_s, wt_s, csc_s, y_s) = refs
    else:
        (xs_ref, b_ref, c_ref, wx_ref, wb_ref, wc_ref, bx_ref, bb_ref, bc_ref,
         dtt_ref, bias_c_ref, a_c_ref, dsk_ref,
         y_ref, fin_ref, rp_s, bt_s, cb_s, cbm_s, cst_s, gt_s, wt_s, csc_s, y_s) = refs
        init_ref = None
    n_chunks = seq_len // CHUNK
    n_pairs = HEADS_PER_GROUP // 2
    lane = lax.broadcasted_iota(jnp.int32, (CHUNK, LANES), 1)

    shift_mats = _conv_shift_mats()
    for c in range(n_chunks):
        rows = slice(c * CHUNK, (c + 1) * CHUNK)
        x_c = _conv_silu_chunk(xs_ref, wx_ref, bx_ref, c, shift_mats)
        y_s[rows, :] = dsk_ref[...] * x_c
        for q in range(n_pairs):
            rp_s[c, q] = _head_pair_blockdiag(x_c[:, q * LANES:(q + 1) * LANES], lane).astype(BF16)
        b_c = _conv_silu_chunk(b_ref, wb_ref, bb_ref, c, shift_mats)
        bt_s[c] = b_c.T
        c_c = _conv_silu_chunk(c_ref, wc_ref, bc_ref, c, shift_mats)
        cb_s[c] = c_c.astype(BF16)
        cbm_s[c] = lax.dot_general(c_c.astype(BF16), b_c.astype(BF16), (((1,), (1,)), ((), ())),
                                   preferred_element_type=F32)

    a_col = a_c_ref[0]
    lane16 = lax.broadcasted_iota(jnp.int32, (2 * HEADS_PER_GROUP, CHUNK), 1)
    fwd_row = lax.broadcasted_iota(jnp.int32, (2 * HEADS_PER_GROUP, CHUNK), 0) < HEADS_PER_GROUP
    pad_rows = jnp.zeros((CHUNK - 2 * HEADS_PER_GROUP, CHUNK), F32)
    for c in range(n_chunks):
        dt_t = _softplus(dtt_ref[0, c] + bias_c_ref[0])
        pre = dt_t * a_col
        suf = pre
        shift = 1
        while shift < CHUNK:
            pre = pre + jnp.where(lane16 >= shift, pltpu.roll(pre, shift, axis=1), 0.0)
            suf = suf + jnp.where(lane16 < CHUNK - shift, pltpu.roll(suf, CHUNK - shift, axis=1), 0.0)
            shift *= 2
        cs_t = jnp.where(fwd_row, pre, suf)
        total = jnp.where(fwd_row, cs_t[:, CHUNK - 1:CHUNK], cs_t[:, 0:1])
        cst_s[c] = cs_t
        gt_s[c] = cs_t - jnp.log(dt_t)
        wt_s[c] = dt_t * jnp.exp(total - cs_t)
        csc_s[c * CHUNK:(c + 1) * CHUNK, :] = jnp.concatenate([cs_t, pad_rows], axis=0).T[
            :, :2 * HEADS_PER_GROUP]

    li = lax.broadcasted_iota(jnp.int32, (CHUNK, CHUNK), 0)
    si = lax.broadcasted_iota(jnp.int32, (CHUNK, CHUNK), 1)
    keep = (li >= si, li <= si)
    last = (CHUNK - 1, 0)

    def scan_chunk(d, c, state):
        rows = pl.ds(pl.multiple_of(c * CHUNK, CHUNK), CHUNK)
        cs = csc_s[rows, :]
        cs_t = cst_s[c]
        g_t = gt_s[c]
        w_t = wt_s[c]
        cb = cbm_s[c]
        b_t = bt_s[c]
        y_off = jnp.dot(cb_s[c], jnp.concatenate(state, axis=1).astype(BF16),
                        preferred_element_type=F32)
        new_state = []
        y_pairs = []
        for q in range(n_pairs):
            m_l, bw_l, e_l = [], [], []
            for r in (2 * q, 2 * q + 1):
                j = d * HEADS_PER_GROUP + r
                cs_l = jnp.broadcast_to(cs[:, j:j + 1], (CHUNK, CHUNK))
                m_l.append((cb * jnp.exp(jnp.where(keep[d], cs_l - g_t[j:j + 1, :], -jnp.inf))
                            ).astype(BF16))
                e_l.append(jnp.exp(cs_l))
                bw_l.append((b_t * w_t[j:j + 1, :]).astype(BF16))
            x_rhs = rp_s[c, q]
            y_diag = jnp.dot(jnp.concatenate(m_l, axis=1), x_rhs, preferred_element_type=F32)
            y_pairs.append(y_diag + y_off[:, q * LANES:(q + 1) * LANES]
                           * jnp.where(lane < SSD_HEADDIM, e_l[0], e_l[1]))
            new = jnp.dot(jnp.concatenate(bw_l, axis=1), x_rhs, preferred_element_type=F32)
            ja = d * HEADS_PER_GROUP + 2 * q
            etot = jnp.exp(cs_t[:, last[d]:last[d] + 1])
            e_row = jnp.where(lane[0:1, :] < SSD_HEADDIM, etot[ja:ja + 1, :], etot[ja + 1:ja + 2, :])
            new_state.append(state[q] * e_row + new)
        y_s[rows, :] += jnp.concatenate(y_pairs, axis=1)
        return tuple(new_state)

    def init_state(d):
        if has_init:
            s0 = init_ref[0, d].reshape(GROUP_W, D_STATE).T
        else:
            s0 = jnp.zeros((D_STATE, GROUP_W), F32)
        return tuple(s0[:, q * LANES:(q + 1) * LANES] for q in range(n_pairs))

    def step(i, carry):
        return (scan_chunk(0, i, carry[0]), scan_chunk(1, n_chunks - 1 - i, carry[1]))

    final = lax.fori_loop(0, n_chunks, step, (init_state(0), init_state(1)))
    for d in range(2):
        fin_ref[0, d] = jnp.concatenate(final[d], axis=1).T.reshape(
            HEADS_PER_GROUP, SSD_HEADDIM, D_STATE)
    y_ref[...] = y_s[...].astype(y_ref.dtype)


def _ssd(xbc, conv_w, conv_b, dt_cols, bias_c, a_c, dsk, init,
         *, n_batch, seq_len, row0):
    sb0 = row0 // seq_len
    n_chunks = seq_len // CHUNK
    n_dh = 2 * HEADS_PER_GROUP
    b_blk0 = D_INNER // D_STATE
    c_blk0 = (D_INNER + SSD_GROUPS * D_STATE) // D_STATE
    has_init = init is not None
    in_specs = [
        pl.BlockSpec((seq_len, GROUP_W), lambda b, g: (sb0 + b, g)),
        pl.BlockSpec((seq_len, D_STATE), lambda b, g: (sb0 + b, b_blk0 + g)),
        pl.BlockSpec((seq_len, D_STATE), lambda b, g: (sb0 + b, c_blk0 + g)),
        pl.BlockSpec((CONV_W, GROUP_W), lambda b, g: (0, g)),
        pl.BlockSpec((CONV_W, D_STATE), lambda b, g: (0, b_blk0 + g)),
        pl.BlockSpec((CONV_W, D_STATE), lambda b, g: (0, c_blk0 + g)),
        pl.BlockSpec((1, GROUP_W), lambda b, g: (0, g)),
        pl.BlockSpec((1, D_STATE), lambda b, g: (0, b_blk0 + g)),
        pl.BlockSpec((1, D_STATE), lambda b, g: (0, c_blk0 + g)),
        pl.BlockSpec((1, n_chunks, n_dh, CHUNK), lambda b, g: (g, sb0 + b, 0, 0)),
        pl.BlockSpec((1, n_dh, 1), lambda b, g: (g, 0, 0)),
        pl.BlockSpec((1, n_dh, 1), lambda b, g: (g, 0, 0)),
        pl.BlockSpec((1, GROUP_W), lambda b, g: (0, g)),
    ]
    args = [xbc, xbc, xbc, conv_w, conv_w, conv_w, conv_b, conv_b, conv_b,
            dt_cols, bias_c, a_c, dsk]
    state_spec = pl.BlockSpec((1, 2, HEADS_PER_GROUP, SSD_HEADDIM, D_STATE),
                              lambda b, g: (b, 0, g, 0, 0))
    if has_init:
        in_specs.append(state_spec)
        args.append(init)
    return pl.pallas_call(
        functools.partial(_ssd_kernel, seq_len=seq_len, has_init=has_init),
        grid=(n_batch, SSD_GROUPS),
        in_specs=in_specs,
        out_specs=[pl.BlockSpec((seq_len, GROUP_W), lambda b, g: (b, g)), state_spec],
        out_shape=[jax.ShapeDtypeStruct((n_batch * seq_len, D_INNER), BF16),
                   jax.ShapeDtypeStruct((n_batch, 2, SSD_HEADS, SSD_HEADDIM, D_STATE), F32)],
        scratch_shapes=[pltpu.VMEM((n_chunks, HEADS_PER_GROUP // 2, 2 * CHUNK, LANES), BF16),
                        pltpu.VMEM((n_chunks, D_STATE, CHUNK), F32),
                        pltpu.VMEM((n_chunks, CHUNK, D_STATE), BF16),
                        pltpu.VMEM((n_chunks, CHUNK, CHUNK), F32),
                        pltpu.VMEM((n_chunks, n_dh, CHUNK), F32),
                        pltpu.VMEM((n_chunks, n_dh, CHUNK), F32),
                        pltpu.VMEM((n_chunks, n_dh, CHUNK), F32),
                        pltpu.VMEM((seq_len, n_dh), F32),
                        pltpu.VMEM((seq_len, GROUP_W), F32)],
        compiler_params=_params("arbitrary", "arbitrary"),
        name="ssd",
    )(*args)


def _ossd_kernel(yc_ref, yl_ref, z_ref, ng_ref, w_ref, g_ref, om_ref, o_ref, a_s, *, n_ctx_tiles):
    i = pl.program_id(0)

    def normalise(y_ref):
        for r0 in range(0, a_s.shape[0], CHUNK):
            rows = slice(r0, r0 + CHUNK)
            u = y_ref[rows, :].astype(F32) * z_ref[rows, :].astype(F32)
            a_s[rows, :] = (_rms(u) * ng_ref[...]).astype(a_s.dtype)

    first = pl.program_id(1) == 0
    pl.when(first & (i < n_ctx_tiles))(lambda: normalise(yc_ref))
    pl.when(first & (i >= n_ctx_tiles))(lambda: normalise(yl_ref))

    p = jnp.dot(a_s[...], w_ref[...], preferred_element_type=F32)
    o_ref[...] = (p * g_ref[...].astype(F32) + om_ref[...]).astype(o_ref.dtype)


def _ossd(y_ctx, y_lat, z, norm_g, w, gates, omla):
    tm, tn = 512, 512
    n_ctx_tiles = N_CTX // tm
    g_blk0 = D_MODEL // tn
    return pl.pallas_call(
        functools.partial(_ossd_kernel, n_ctx_tiles=n_ctx_tiles),
        grid=(N_TOK // tm, D_MODEL // tn),
        in_specs=[pl.BlockSpec((tm, D_INNER), lambda i, j: (jnp.minimum(i, n_ctx_tiles - 1), 0)),
                  pl.BlockSpec((tm, D_INNER), lambda i, j: (jnp.maximum(i - n_ctx_tiles, 0), 0)),
                  pl.BlockSpec((tm, D_INNER), lambda i, j: (i, 0)),
                  pl.BlockSpec((1, D_INNER), lambda i, j: (0, 0)),
                  pl.BlockSpec((D_INNER, tn), lambda i, j: (0, j)),
                  pl.BlockSpec((tm, tn), lambda i, j: (i, g_blk0 + j)),
                  pl.BlockSpec((tm, tn), lambda i, j: (i, j))],
        out_specs=pl.BlockSpec((tm, tn), lambda i, j: (i, j)),
        out_shape=jax.ShapeDtypeStruct((N_TOK, D_MODEL), BF16),
        scratch_shapes=[pltpu.VMEM((tm, D_INNER), BF16)],
        compiler_params=_params("arbitrary", "arbitrary"),
        name="ossd_merge",
    )(y_ctx, y_lat, z, norm_g.reshape(1, D_INNER), w, gates, omla)


def _pack_halves(v):
    n = v.shape[1] // 2
    lo = lax.bitcast_convert_type(v[:, :n].astype(BF16).astype(F32), jnp.uint32)
    hi = lax.bitcast_convert_type(v[:, n:].astype(BF16).astype(F32), jnp.uint32)
    return (lo >> 16) | (hi & jnp.uint32(0xFFFF0000))


def _unpack_halves(w):
    lo = lax.bitcast_convert_type(w << 16, F32).astype(BF16)
    hi = lax.bitcast_convert_type(w & jnp.uint32(0xFFFF0000), F32).astype(BF16)
    return lo, hi


def _outproj_kernel(a_ref, w_ref, xc_ref, xl_ref, pg_ref, g1_ref, p2_ref, sc_ref, sh_ref, wr_ref,
                    x1_ref, h2_ref, lg_ref, *, n_ctx_tiles):
    def run(x_ref):
        out = jnp.dot(a_ref[...], w_ref[...], preferred_element_type=F32)
        x1 = x_ref[...] + g1_ref[0] * (_rms(out) * pg_ref[...])
        x1_ref[...] = x1
        h2 = (_rms(x1) * p2_ref[...]) * (1.0 + sc_ref[0]) + sh_ref[0]
        h2_ref[...] = _pack_halves(h2)
        lg_ref[...] = jnp.dot(h2.astype(BF16), wr_ref[...], preferred_element_type=F32)

    i = pl.program_id(0)
    pl.when(i < n_ctx_tiles)(lambda: run(xc_ref))
    pl.when(i >= n_ctx_tiles)(lambda: run(xl_ref))


def _outproj(merged, w_out, x_ctx, x_lat, post1_g, pre2_g, mod, w_router_pad):
    tm = 256
    r = _mod_row(tm)
    row = lambda i: (i, 0)
    one = lambda i: (0, 0)
    ctx, lat = _ctx_lat_specs(tm, D_MODEL)
    return pl.pallas_call(
        functools.partial(_outproj_kernel, n_ctx_tiles=N_CTX // tm),
        grid=(N_TOK // tm,),
        in_specs=[pl.BlockSpec((tm, D_MODEL), row),
                  pl.BlockSpec((D_MODEL, D_MODEL), one),
                  ctx, lat,
                  pl.BlockSpec((1, D_MODEL), one),
                  pl.BlockSpec((1, 1, D_MODEL), lambda i: (r(i), 0, 2)),
                  pl.BlockSpec((1, D_MODEL), one),
                  pl.BlockSpec((1, 1, D_MODEL), lambda i: (r(i), 0, 4)),
                  pl.BlockSpec((1, 1, D_MODEL), lambda i: (r(i), 0, 3)),
                  pl.BlockSpec((D_MODEL, LANES), one)],
        out_specs=[pl.BlockSpec((tm, D_MODEL), row),
                   pl.BlockSpec((tm, D_MODEL // 2), row),
                   pl.BlockSpec((tm, LANES), row)],
        out_shape=[jax.ShapeDtypeStruct((N_TOK, D_MODEL), F32),
                   jax.ShapeDtypeStruct((N_TOK, D_MODEL // 2), jnp.uint32),
                   jax.ShapeDtypeStruct((N_TOK, LANES), F32)],
        compiler_params=_params("arbitrary"),
        name="outproj",
    )(merged, w_out, x_ctx, x_lat, post1_g.reshape(1, D_MODEL), mod, pre2_g.reshape(1, D_MODEL),
      mod, mod, w_router_pad)


GATHER_STEP = 1


def _ffn_kernel(idx_ref, pos_ref, h2_hbm, wg_ref, wu_ref, wd_ref, gv_ref, ys_hbm,
                xbuf, acc_ref, gsem, ssem):
    e = pl.program_id(0)
    f = pl.program_id(1)
    n_e = pl.num_programs(0)
    n_f = pl.num_programs(1)
    n_rows = xbuf.shape[1]
    slot = lax.rem(e, 2)

    def gather_rows(expert, buf):
        def body(i, carry):
            pltpu.make_async_copy(h2_hbm.at[pl.ds(idx_ref[expert * n_rows + i], 1), :],
                                  xbuf.at[buf, pl.ds(i, 1), :], gsem.at[buf]).start()
            return carry
        lax.fori_loop(0, n_rows, body, 0, unroll=8)

    def scatter_rows(expert, buf):
        def body(i, carry):
            pltpu.make_async_copy(xbuf.at[buf, pl.ds(i, 1), :],
                                  ys_hbm.at[pl.ds(pos_ref[expert * n_rows + i], 1), :],
                                  ssem.at[buf]).start()
            return carry
        lax.fori_loop(0, n_rows, body, 0, unroll=8)

    def wait_gather(buf):
        pltpu.make_async_copy(h2_hbm.at[pl.ds(0, n_rows), :], xbuf.at[buf], gsem.at[buf]).wait()

    def wait_scatter(buf):
        pltpu.make_async_copy(xbuf.at[buf], ys_hbm.at[pl.ds(0, n_rows), :], ssem.at[buf]).wait()

    @pl.when((e == 0) & (f == 0))
    def _():
        gather_rows(0, 0)

    @pl.when(f == 0)
    def _():
        wait_gather(slot)

    wg = wg_ref[0].astype(BF16)
    wu = wu_ref[0].astype(BF16)
    wd = wd_ref[0].astype(BF16)
    for r0 in range(0, n_rows, FFN_ROWS):
        rows = slice(r0, r0 + FFN_ROWS)
        x = jnp.concatenate(_unpack_halves(xbuf[slot, rows, :]), axis=1)
        g = jnp.dot(x, wg, preferred_element_type=F32)
        u = jnp.dot(x, wu, preferred_element_type=F32)
        hid = (_silu(g) * u).astype(BF16)
        part = jnp.dot(hid, wd, preferred_element_type=F32)

        @pl.when(f == 0)
        def _():
            acc_ref[rows, :] = part

        @pl.when(f > 0)
        def _():
            acc_ref[rows, :] += part

        @pl.when(f == n_f - 1)
        def _():
            xbuf[slot, rows, :] = _pack_halves(acc_ref[rows, :] * gv_ref[0, rows, :])

    @pl.when(f == n_f - 1)
    def _():
        scatter_rows(e, slot)

    @pl.when((f == GATHER_STEP) & (e + 1 < n_e))
    def _():
        @pl.when(e >= 1)
        def _():
            wait_scatter(1 - slot)
        gather_rows(e + 1, 1 - slot)

    @pl.when((e == n_e - 1) & (f == n_f - 1))
    def _():
        wait_scatter(1 - slot)
        wait_scatter(slot)


def _ffn(idx, pos, h2p, w_gate, w_up, w_down, gval):
    n_rows = gval.shape[1]
    tf = 256
    n_f = D_EXPERT // tf
    assert N_EXPERTS >= 2 and n_f > GATHER_STEP
    return pl.pallas_call(
        _ffn_kernel,
        grid_spec=pltpu.PrefetchScalarGridSpec(
            num_scalar_prefetch=2,
            grid=(N_EXPERTS, n_f),
            in_specs=[pl.BlockSpec(memory_space=pl.ANY),
                      pl.BlockSpec((1, D_MODEL, tf), lambda e, f, i, p: (e, 0, f)),
                      pl.BlockSpec((1, D_MODEL, tf), lambda e, f, i, p: (e, 0, f)),
                      pl.BlockSpec((1, tf, D_MODEL), lambda e, f, i, p: (e, f, 0)),
                      pl.BlockSpec((1, n_rows, 1), lambda e, f, i, p: (e, 0, 0))],
            out_specs=pl.BlockSpec(memory_space=pl.ANY),
            scratch_shapes=[pltpu.VMEM((2, n_rows, D_MODEL // 2), jnp.uint32),
                            pltpu.VMEM((n_rows, D_MODEL), F32),
                            pltpu.SemaphoreType.DMA((2,)),
                            pltpu.SemaphoreType.DMA((2,))]),
        out_shape=jax.ShapeDtypeStruct((N_EXPERTS * n_rows, D_MODEL // 2), jnp.uint32),
        compiler_params=_params("arbitrary", "arbitrary"),
        name="expert_ffn",
    )(idx, pos, h2p, w_gate, w_up, w_down, gval)


COMBINE_TILE = 256


def _combine_kernel(wt_ref, wb_ref, wfirst_ref, wlast_ref, wvalid_ref,
                    ys_ref, s0_ref, s1_ref, x1_ref, pg_ref, g2_ref, oc_ref, ol_ref, acc_ref,
                    *, n_ctx_tiles):
    w = pl.program_id(0)
    half = acc_ref.shape[1] // 2

    @pl.when(wfirst_ref[w] == 1)
    def _():
        acc_ref[...] = jnp.zeros_like(acc_ref)

    @pl.when(wvalid_ref[w] == 1)
    def _():
        row = wb_ref[w] * COMBINE_TILE + lax.broadcasted_iota(
            jnp.int32, (COMBINE_TILE, COMBINE_TILE), 1)
        own = (row >= s0_ref[...]) & (row < s1_ref[...])
        sel = jnp.where(own, 1.0, 0.0).astype(BF16)
        lo, hi = _unpack_halves(ys_ref[...])
        acc_ref[:, :half] += jnp.dot(sel, lo, preferred_element_type=F32)
        acc_ref[:, half:] += jnp.dot(sel, hi, preferred_element_type=F32)

    def finish(o_ref):
        o_ref[...] = x1_ref[...] + g2_ref[0] * (_rms(acc_ref[...]) * pg_ref[...])

    done = wlast_ref[w] == 1
    pl.when(done & (wt_ref[w] < n_ctx_tiles))(lambda: finish(oc_ref))
    pl.when(done & (wt_ref[w] >= n_ctx_tiles))(lambda: finish(ol_ref))


def _combine(plan, ys, seg0, seg1, x1, post2_g, mod, n_ctx):
    n_tok = x1.shape[0]
    tile = COMBINE_TILE
    n_ctx_tiles = n_ctx // tile
    r = _mod_row(tile)
    n_work = plan[0].shape[0]
    tok = lambda w, wt, wb, wf, wl, wv: (wt[w], 0)
    return pl.pallas_call(
        functools.partial(_combine_kernel, n_ctx_tiles=n_ctx_tiles),
        grid_spec=pltpu.PrefetchScalarGridSpec(
            num_scalar_prefetch=5,
            grid=(n_work,),
            in_specs=[pl.BlockSpec((tile, D_MODEL // 2), lambda w, wt, wb, wf, wl, wv: (wb[w], 0)),
                      pl.BlockSpec((tile, 1), tok),
                      pl.BlockSpec((tile, 1), tok),
                      pl.BlockSpec((tile, D_MODEL), tok),
                      pl.BlockSpec((1, D_MODEL), lambda w, wt, wb, wf, wl, wv: (0, 0)),
                      pl.BlockSpec((1, 1, D_MODEL), lambda w, wt, wb, wf, wl, wv: (r(wt[w]), 0, 5))],
            out_specs=[pl.BlockSpec((tile, D_MODEL), lambda w, wt, wb, wf, wl, wv:
                                    (jnp.minimum(wt[w], n_ctx_tiles - 1), 0)),
                       pl.BlockSpec((tile, D_MODEL), lambda w, wt, wb, wf, wl, wv:
                                    (jnp.maximum(wt[w] - n_ctx_tiles, 0), 0))],
            scratch_shapes=[pltpu.VMEM((tile, D_MODEL), F32)]),
        out_shape=[jax.ShapeDtypeStruct((n_ctx, D_MODEL), F32),
                   jax.ShapeDtypeStruct((n_tok - n_ctx, D_MODEL), F32)],
        compiler_params=_params("arbitrary"),
        name="combine",
    )(*plan, ys, seg0, seg1, x1, post2_g.reshape(1, D_MODEL), mod)


def _moe_plan(idx, n_tok):
    n_e, n_r = idx.shape
    tile = COMBINE_TILE
    chosen = jnp.zeros((n_e, n_tok), jnp.int32).at[jnp.arange(n_e)[:, None], idx].set(1)
    seg = jnp.concatenate([jnp.zeros((1,), jnp.int32), jnp.cumsum(chosen.sum(0))])
    before = jnp.cumsum(chosen, axis=0) - chosen
    pos = seg[idx] + jnp.take_along_axis(before, idx, axis=1)

    n_tiles = n_tok // tile
    n_blocks = n_e * n_r // tile
    off = seg[::tile]
    first_blk = jnp.minimum(off[:-1] // tile, n_blocks - 1)
    last_blk = jnp.maximum((off[1:] - 1) // tile, first_blk)
    n_blk = last_blk - first_blk + 1
    start = jnp.cumsum(n_blk) - n_blk
    total = n_blk.sum()
    w = jnp.arange(n_tiles + n_blocks, dtype=jnp.int32)
    valid = w < total
    wt = jnp.clip(jnp.searchsorted(start, w, side="right") - 1, 0, n_tiles - 1)
    wt = jnp.where(valid, wt, n_tiles - 1).astype(jnp.int32)
    wb = jnp.where(valid, first_blk[wt] + (w - start[wt]), n_blocks - 1).astype(jnp.int32)
    first = (valid & (w == start[wt])).astype(jnp.int32)
    last = (valid & (w == start[wt] + n_blk[wt] - 1)).astype(jnp.int32)
    return pos.astype(jnp.int32), seg, (wt, wb, first, last, valid.astype(jnp.int32))


def _rope_tables():
    rows = DEC_SEQ // GRID_W
    row = jnp.repeat(jnp.arange(rows), GRID_W).astype(F32)
    col = jnp.tile(jnp.arange(GRID_W), rows).astype(F32)
    half = QK_ROPE // 2
    inv = ROPE_THETA ** (-jnp.arange(0, half, 2, dtype=F32) / half)
    ang_r = row[:, None] * inv[None, :]
    ang_c = col[:, None] * inv[None, :]
    ang = jnp.concatenate([ang_r, ang_r, ang_c, ang_c], axis=-1)
    cos, sin = jnp.cos(ang), jnp.sin(ang)
    ident = 256
    one = lambda n: jnp.ones((DEC_SEQ, n), F32)
    zero = lambda n: jnp.zeros((DEC_SEQ, n), F32)

    def table(parts_cos, parts_sin, width):
        c = jnp.concatenate([jnp.ones((ident, width), F32), jnp.concatenate(parts_cos, -1)], 0)
        s = jnp.concatenate([jnp.zeros((ident, width), F32), jnp.concatenate(parts_sin, -1)], 0)
        return c, s

    cos_k, sin_k = table([cos, one(LANES - QK_ROPE)], [sin, zero(LANES - QK_ROPE)], LANES)
    pad = Q_HEAD_PAD - QK_NOPE - QK_ROPE
    cos_q, sin_q = table([one(QK_NOPE), cos, one(pad)], [zero(QK_NOPE), sin, zero(pad)], Q_HEAD_PAD)
    q_scale = (QK_NOPE + QK_ROPE) ** -0.5 * np.log2(np.e)
    return cos_k, sin_k, cos_q * q_scale, sin_q * q_scale


def _route(logits, n_tok):
    cap = CAP_FACTOR * n_tok // N_EXPERTS
    aff = jax.nn.softmax(logits, axis=-1)
    return lax.top_k(aff.T, cap)


def kernel(x_prompt, x_sample, cache_ckv, cache_krope, state_ssm, c, c_ctx, w_mod, b_mod,
           pre1_g, post1_g, pre2_g, post2_g, w_in, q_norm_g, w_q_b, kv_norm_g, w_kv_b, w_o_mla,
           conv_w, conv_b, dt_bias, a_log, d_skip, ssd_norm_g, w_o_ssd, w_out, w_router,
           w_e_gate, w_e_up, w_e_down):
    l = 0
    x_ctx = x_prompt.reshape(N_CTX, D_MODEL)
    x_lat = x_sample.reshape(N_LAT, D_MODEL)

    n_mod = 1 + DEC_BATCH
    cvec = jnp.concatenate([c_ctx[None, :], c, jnp.zeros((16 - n_mod, D_MODEL), F32)], 0)
    mod = _adaln(cvec, w_mod[l], b_mod[l]).reshape(16, 1, 6 * D_MODEL)

    wi = w_in[l]
    o_z = Q_RANK + KV_RANK + QK_ROPE
    o_xbc = o_z + D_INNER
    o_dt = o_xbc + XBC_DIM
    o_gate = o_dt + 2 * SSD_HEADS
    w_small = jnp.concatenate([wi[:, :o_z], jnp.zeros((D_MODEL, LANES - QK_ROPE), F32),
                               wi[:, o_dt:o_gate]], axis=1).astype(BF16)
    w_z = wi[:, o_z:o_xbc].astype(BF16)
    w_xbc = wi[:, o_xbc:o_dt].astype(BF16)
    w_gates = wi[:, o_gate:].astype(BF16)
    wq = w_q_b[l].reshape(Q_RANK, N_HEADS, QK_NOPE + QK_ROPE)
    wq = jnp.concatenate([wq, jnp.zeros((Q_RANK, N_HEADS, Q_HEAD_PAD - QK_NOPE - QK_ROPE), F32)],
                         axis=-1).reshape(Q_RANK, N_HEADS * Q_HEAD_PAD).astype(BF16)
    cos_k, sin_k, cos_q, sin_q = _rope_tables()

    h1 = _prenorm(x_ctx, x_lat, pre1_g[l], mod, 1, 0)
    cqn, ckv, kr, dt_raw = _inproj_small(h1, w_small, q_norm_g[l], kv_norm_g[l], cos_k, sin_k)
    z = _mm(h1, w_z, tm=512, tn=1024, out_dtype=BF16, act="silu", name="inproj_z")
    xbc = _mm(h1, w_xbc, tm=512, tn=1024, out_dtype=BF16, name="inproj_xbc")
    gates = _mm(h1, w_gates, tm=512, tn=1024, out_dtype=BF16, act="sigmoid", name="inproj_gates")

    q = _qproj(cqn, wq, cos_q, sin_q)
    ckv_b = ckv.astype(BF16)
    kr_b = kr.astype(BF16)
    keys_ctx = ckv_b[:N_CTX]
    keys_lat = jnp.concatenate([cache_ckv[:, l].astype(BF16),
                                ckv_b[N_CTX:].reshape(DEC_BATCH, DEC_SEQ, KV_RANK)], axis=1)
    keys_lat = keys_lat.reshape(DEC_BATCH * (PAST_LEN + DEC_SEQ), KV_RANK)
    kr_cache = jnp.concatenate([cache_krope[:, l].astype(BF16),
                                jnp.zeros((DEC_BATCH, PAST_LEN, LANES - QK_ROPE), BF16)], axis=-1)
    kr_lat = jnp.concatenate([kr_cache, kr_b[N_CTX:].reshape(DEC_BATCH, DEC_SEQ, LANES)], axis=1)
    kr_lat = kr_lat.reshape(DEC_BATCH * (PAST_LEN + DEC_SEQ), LANES)
    wkv = w_kv_b[l].reshape(KV_RANK, N_HEADS, QK_NOPE + V_DIM)
    wk_t = wkv[:, :, :QK_NOPE].reshape(KV_RANK, N_HEADS * QK_NOPE).T.astype(BF16)
    wv = wkv[:, :, QK_NOPE:].reshape(KV_RANK, N_HEADS * V_DIM).astype(BF16)
    knt_ctx = _mm_nt(wk_t, keys_ctx, tn=512, out_dtype=BF16, name="knt_ctx")
    knt_lat = _mm_nt(wk_t, keys_lat, tn=512, out_dtype=BF16, name="knt_lat")
    v_ctx = _mm(keys_ctx, wv, tm=512, tn=2048, out_dtype=BF16, name="v_ctx")
    v_lat = _mm(keys_lat, wv, tm=512, tn=2048, out_dtype=BF16, name="v_lat")
    o_ctx = _attention(q, knt_ctx, kr_b[:N_CTX].T, v_ctx, n_batch=BATCH, lq=SEQ, lk=SEQ, tq=SEQ,
                       heads=N_HEADS, q_row0=0)
    o_lat = _attention(q, knt_lat, kr_lat.T, v_lat, n_batch=DEC_BATCH, lq=DEC_SEQ,
                       lk=PAST_LEN + DEC_SEQ, tq=512, heads=2, q_row0=N_CTX)
    omla = _omla(o_ctx, o_lat, w_o_mla[l].astype(BF16), gates)

    n_dh = 2 * HEADS_PER_GROUP
    dt4 = dt_raw.reshape(N_TOK, 2, SSD_GROUPS, HEADS_PER_GROUP)
    dt_rows = dt4.transpose(2, 0, 1, 3).reshape(SSD_GROUPS, N_TOK, n_dh)
    dt_cols = dt_rows.reshape(SSD_GROUPS, N_TOK // CHUNK, CHUNK, n_dh).transpose(0, 1, 3, 2)
    per_group = lambda v: v.reshape(2, SSD_GROUPS, HEADS_PER_GROUP).transpose(1, 0, 2).reshape(
        SSD_GROUPS, n_dh)
    bias_g = per_group(dt_bias[l])
    a_g = per_group(-jnp.exp(a_log[l]))
    dsk = jnp.repeat(d_skip[l], SSD_HEADDIM).reshape(1, D_INNER)
    ssd_args = (xbc, conv_w[l], conv_b[l].reshape(1, XBC_DIM), dt_cols, bias_g[:, :, None],
                a_g[:, :, None], dsk)
    y_ctx, ssm_ctx = _ssd(*ssd_args, None, n_batch=BATCH, seq_len=SEQ, row0=0)
    y_lat, _ = _ssd(*ssd_args, state_ssm[:, l], n_batch=DEC_BATCH, seq_len=DEC_SEQ, row0=N_CTX)
    merged = _ossd(y_ctx, y_lat, z, ssd_norm_g[l], w_o_ssd[l].astype(BF16), gates, omla)

    w_router_pad = jnp.concatenate([w_router[l], jnp.zeros((D_MODEL, LANES - N_EXPERTS), F32)],
                                   axis=1).astype(BF16)
    x1, h2p, logits = _outproj(merged, w_out[l].astype(BF16), x_ctx, x_lat, post1_g[l], pre2_g[l],
                               mod, w_router_pad)

    logits = logits[:, :N_EXPERTS]
    gv_c, idx_c = _route(logits[:N_CTX], N_CTX)
    gv_l, idx_l = _route(logits[N_CTX:], N_LAT)
    idx = jnp.concatenate([idx_c, idx_l + N_CTX], axis=1)
    gval = jnp.concatenate([gv_c, gv_l], axis=1)
    pos, seg, plan = _moe_plan(idx, N_TOK)
    ys = _ffn(idx.reshape(-1), pos.reshape(-1), h2p, w_e_gate[l], w_e_up[l], w_e_down[l],
              gval[..., None])
    out_ctx, out_lat = _combine(plan, ys, seg[:-1].reshape(N_TOK, 1), seg[1:].reshape(N_TOK, 1), x1,
                                post2_g[l], mod, N_CTX)

    y_p = out_ctx.reshape(BATCH, SEQ, D_MODEL)
    y_s = out_lat.reshape(DEC_BATCH, DEC_SEQ, D_MODEL)
    new_ckv = ckv[:N_CTX].reshape(BATCH, 1, SEQ, KV_RANK)
    new_krope = kr[:N_CTX, :QK_ROPE].reshape(BATCH, 1, SEQ, QK_ROPE)
    new_ssm = ssm_ctx[:, None]
    return (y_p, y_s, new_ckv, new_krope, new_ssm)
```

```python
import functools

import jax
import jax.numpy as jnp
import numpy as np
from jax import lax
from jax.experimental import pallas as pl
from jax.experimental.pallas import tpu as pltpu

F32 = jnp.float32
BF16 = jnp.bfloat16

D_MODEL = 2048
BATCH = 16
SEQ = 256
DEC_BATCH = 8
DEC_SEQ = 1024
PAST_LEN = 512
GRID_W = 64
N_HEADS = 16
Q_RANK = 512
KV_RANK = 512
QK_NOPE = 128
QK_ROPE = 64
V_DIM = 128
ROPE_THETA = 10000.0
D_INNER = 2 * D_MODEL
SSD_HEADDIM = 64
SSD_HEADS = D_INNER // SSD_HEADDIM
SSD_GROUPS = 8
HEADS_PER_GROUP = SSD_HEADS // SSD_GROUPS
D_STATE = 128
CONV_W = 5
CHUNK = 128
XBC_DIM = D_INNER + 2 * SSD_GROUPS * D_STATE
N_EXPERTS = 16
CAP_FACTOR = 2
D_EXPERT = 1536
EPS = 1e-6

N_CTX = BATCH * SEQ
N_LAT = DEC_BATCH * DEC_SEQ
N_TOK = N_CTX + N_LAT
Q_HEAD_PAD = 256
GROUP_W = HEADS_PER_GROUP * SSD_HEADDIM
LANES = 128
VMEM_LIMIT = 56 * 1024 * 1024
CONV_PAD = 16
FFN_ROWS = 512


def _params(*sem):
    return pltpu.CompilerParams(dimension_semantics=sem, vmem_limit_bytes=VMEM_LIMIT)


def _rms(x):
    return x * lax.rsqrt(jnp.mean(x * x, axis=-1, keepdims=True) + EPS)


def _silu(x):
    return x * jax.nn.sigmoid(x)


def _mod_row(tm):
    n_ctx_tiles = N_CTX // tm
    per_lat = DEC_SEQ // tm
    return lambda i: jnp.where(i < n_ctx_tiles, 0, 1 + (i - n_ctx_tiles) // per_lat)


def _rope_row(tm):
    n_ctx_tiles = N_CTX // tm
    per_lat = DEC_SEQ // tm
    return lambda i: jnp.where(i < n_ctx_tiles, 0, 1 + (i - n_ctx_tiles) % per_lat)


def _rot_half(x):
    w = x.shape[1]
    nxt = pltpu.roll(x, w - 16, axis=1)
    prv = pltpu.roll(x, 16, axis=1)
    lane = lax.broadcasted_iota(jnp.int32, x.shape, 1)
    return jnp.where((lane % 32) < 16, -nxt, prv)


def _adaln_kernel(c_ref, w_ref, b_ref, o_ref):
    a = _silu(c_ref[...]).astype(BF16)
    o_ref[...] = jnp.dot(a, w_ref[...].astype(BF16), preferred_element_type=F32) + b_ref[...]


def _adaln(cvec, w_mod, b_mod):
    rows = cvec.shape[0]
    n = w_mod.shape[1]
    tn = 1024
    return pl.pallas_call(
        _adaln_kernel,
        grid=(n // tn,),
        in_specs=[pl.BlockSpec((rows, D_MODEL), lambda j: (0, 0)),
                  pl.BlockSpec((D_MODEL, tn), lambda j: (0, j)),
                  pl.BlockSpec((1, tn), lambda j: (0, j))],
        out_specs=pl.BlockSpec((rows, tn), lambda j: (0, j)),
        out_shape=jax.ShapeDtypeStruct((rows, n), F32),
        compiler_params=_params("arbitrary"),
        name="adaln",
    )(cvec, w_mod, b_mod.reshape(1, n))


def _ctx_lat_specs(tm, width):
    n_ctx_tiles = N_CTX // tm
    ctx = pl.BlockSpec((tm, width), lambda i: (jnp.minimum(i, n_ctx_tiles - 1), 0))
    lat = pl.BlockSpec((tm, width), lambda i: (jnp.maximum(i - n_ctx_tiles, 0), 0))
    return ctx, lat


def _prenorm_kernel(xc_ref, xl_ref, g_ref, sc_ref, sh_ref, o_ref, *, n_ctx_tiles):
    def run(x_ref):
        y = _rms(x_ref[...]) * g_ref[...]
        o_ref[...] = (y * (1.0 + sc_ref[0]) + sh_ref[0]).astype(o_ref.dtype)

    i = pl.program_id(0)
    pl.when(i < n_ctx_tiles)(lambda: run(xc_ref))
    pl.when(i >= n_ctx_tiles)(lambda: run(xl_ref))


def _prenorm(x_ctx, x_lat, gain, mod, sc_blk, sh_blk):
    tm = 256
    r = _mod_row(tm)
    ctx, lat = _ctx_lat_specs(tm, D_MODEL)
    return pl.pallas_call(
        functools.partial(_prenorm_kernel, n_ctx_tiles=N_CTX // tm),
        grid=(N_TOK // tm,),
        in_specs=[ctx, lat,
                  pl.BlockSpec((1, D_MODEL), lambda i: (0, 0)),
                  pl.BlockSpec((1, 1, D_MODEL), lambda i: (r(i), 0, sc_blk)),
                  pl.BlockSpec((1, 1, D_MODEL), lambda i: (r(i), 0, sh_blk))],
        out_specs=pl.BlockSpec((tm, D_MODEL), lambda i: (i, 0)),
        out_shape=jax.ShapeDtypeStruct((N_TOK, D_MODEL), BF16),
        compiler_params=_params("arbitrary"),
        name="prenorm",
    )(x_ctx, x_lat, gain.reshape(1, D_MODEL), mod, mod)


def _inproj_small_kernel(h_ref, w_ref, qg_ref, kvg_ref, cos_ref, sin_ref,
                         cqn_ref, ckv_ref, kr_ref, dt_ref):
    p = jnp.dot(h_ref[...], w_ref[...], preferred_element_type=F32)
    cqn_ref[...] = (_rms(p[:, :Q_RANK]) * qg_ref[...]).astype(cqn_ref.dtype)
    ckv_ref[...] = _rms(p[:, Q_RANK:Q_RANK + KV_RANK]) * kvg_ref[...]
    kr = p[:, Q_RANK + KV_RANK:Q_RANK + KV_RANK + LANES]
    kr_ref[...] = kr * cos_ref[...] + _rot_half(kr) * sin_ref[...]
    dt_ref[...] = p[:, Q_RANK + KV_RANK + LANES:]


def _inproj_small(h, w, q_norm_g, kv_norm_g, cos_k, sin_k):
    tm = 256
    n = w.shape[1]
    rr = _rope_row(tm)
    row = lambda i: (i, 0)
    return pl.pallas_call(
        _inproj_small_kernel,
        grid=(N_TOK // tm,),
        in_specs=[pl.BlockSpec((tm, D_MODEL), row),
                  pl.BlockSpec((D_MODEL, n), lambda i: (0, 0)),
                  pl.BlockSpec((1, Q_RANK), lambda i: (0, 0)),
                  pl.BlockSpec((1, KV_RANK), lambda i: (0, 0)),
                  pl.BlockSpec((tm, LANES), lambda i: (rr(i), 0)),
                  pl.BlockSpec((tm, LANES), lambda i: (rr(i), 0))],
        out_specs=[pl.BlockSpec((tm, Q_RANK), row),
                   pl.BlockSpec((tm, KV_RANK), row),
                   pl.BlockSpec((tm, LANES), row),
                   pl.BlockSpec((tm, LANES), row)],
        out_shape=[jax.ShapeDtypeStruct((N_TOK, Q_RANK), BF16),
                   jax.ShapeDtypeStruct((N_TOK, KV_RANK), F32),
                   jax.ShapeDtypeStruct((N_TOK, LANES), F32),
                   jax.ShapeDtypeStruct((N_TOK, LANES), F32)],
        compiler_params=_params("arbitrary"),
        name="inproj_small",
    )(h, w, q_norm_g.reshape(1, Q_RANK), kv_norm_g.reshape(1, KV_RANK), cos_k, sin_k)


def _mm_kernel(a_ref, w_ref, o_ref, *, act):
    p = jnp.dot(a_ref[...], w_ref[...], preferred_element_type=F32)
    if act == "sigmoid":
        p = jax.nn.sigmoid(p)
    elif act == "silu":
        p = _silu(p)
    o_ref[...] = p.astype(o_ref.dtype)


def _mm(a, w, *, tm, tn, out_dtype, act=None, name="mm"):
    m, k = a.shape
    n = w.shape[1]
    return pl.pallas_call(
        functools.partial(_mm_kernel, act=act),
        grid=(n // tn, m // tm),
        in_specs=[pl.BlockSpec((tm, k), lambda j, i: (i, 0)),
                  pl.BlockSpec((k, tn), lambda j, i: (0, j))],
        out_specs=pl.BlockSpec((tm, tn), lambda j, i: (i, j)),
        out_shape=jax.ShapeDtypeStruct((m, n), out_dtype),
        compiler_params=_params("arbitrary", "arbitrary"),
        name=name,
    )(a, w)


def _mm_nt_kernel(a_ref, b_ref, o_ref):
    o_ref[...] = lax.dot_general(a_ref[...], b_ref[...], (((1,), (1,)), ((), ())),
                                 preferred_element_type=F32).astype(o_ref.dtype)


def _mm_nt(a, b, *, tn, out_dtype, name):
    m, k = a.shape
    n = b.shape[0]
    return pl.pallas_call(
        _mm_nt_kernel,
        grid=(n // tn,),
        in_specs=[pl.BlockSpec((m, k), lambda j: (0, 0)),
                  pl.BlockSpec((tn, k), lambda j: (j, 0))],
        out_specs=pl.BlockSpec((m, tn), lambda j: (0, j)),
        out_shape=jax.ShapeDtypeStruct((m, n), out_dtype),
        compiler_params=_params("arbitrary"),
        name=name,
    )(a, b)


def _qproj_kernel(a_ref, w_ref, cos_ref, sin_ref, o_ref):
    p = jnp.dot(a_ref[...], w_ref[...], preferred_element_type=F32)
    cos = cos_ref[...]
    sin = sin_ref[...]
    for h in range(N_HEADS):
        nope = slice(h * Q_HEAD_PAD, h * Q_HEAD_PAD + QK_NOPE)
        rope = slice(h * Q_HEAD_PAD + QK_NOPE, (h + 1) * Q_HEAD_PAD)
        o_ref[:, nope] = (p[:, nope] * cos[:, :QK_NOPE]).astype(o_ref.dtype)
        pr = p[:, rope]
        o_ref[:, rope] = (pr * cos[:, QK_NOPE:] + _rot_half(pr) * sin[:, QK_NOPE:]).astype(o_ref.dtype)


def _qproj(cqn, wq, cos_q, sin_q):
    tm = 256
    n = wq.shape[1]
    rr = _rope_row(tm)
    return pl.pallas_call(
        _qproj_kernel,
        grid=(N_TOK // tm,),
        in_specs=[pl.BlockSpec((tm, Q_RANK), lambda i: (i, 0)),
                  pl.BlockSpec((Q_RANK, n), lambda i: (0, 0)),
                  pl.BlockSpec((tm, Q_HEAD_PAD), lambda i: (rr(i), 0)),
                  pl.BlockSpec((tm, Q_HEAD_PAD), lambda i: (rr(i), 0))],
        out_specs=pl.BlockSpec((tm, n), lambda i: (i, 0)),
        out_shape=jax.ShapeDtypeStruct((N_TOK, n), BF16),
        compiler_params=_params("arbitrary"),
        name="qproj",
    )(cqn, wq, cos_q, sin_q)


ATTN_Q_ROWS = 256


def _attn_kernel(q_ref, knt_ref, krt_ref, v_ref, o_ref, *, heads):
    kr_t = krt_ref[...]
    for h in range(heads):
        k_t = jnp.concatenate([knt_ref[h * QK_NOPE:(h + 1) * QK_NOPE, :], kr_t], axis=0)
        v = v_ref[:, h * V_DIM:(h + 1) * V_DIM]
        for r0 in range(0, q_ref.shape[0], ATTN_Q_ROWS):
            rows = slice(r0, r0 + ATTN_Q_ROWS)
            q = q_ref[rows, h * Q_HEAD_PAD:(h + 1) * Q_HEAD_PAD]
            s = jnp.dot(q, k_t, preferred_element_type=F32)
            p = jnp.exp2(s - jnp.max(s, axis=-1, keepdims=True))
            l = jnp.sum(p, axis=-1, keepdims=True)
            o = jnp.dot(p.astype(BF16), v, preferred_element_type=F32)
            o_ref[rows, h * V_DIM:(h + 1) * V_DIM] = (o / l).astype(o_ref.dtype)


def _attention(q, kn_t, kr_t, v, *, n_batch, lq, lk, tq, heads, q_row0):
    nq = lq // tq
    qb0 = q_row0 // tq
    return pl.pallas_call(
        functools.partial(_attn_kernel, heads=heads),
        grid=(n_batch, N_HEADS // heads, nq),
        in_specs=[pl.BlockSpec((tq, heads * Q_HEAD_PAD), lambda b, h, i: (qb0 + b * nq + i, h)),
                  pl.BlockSpec((heads * QK_NOPE, lk), lambda b, h, i: (h, b)),
                  pl.BlockSpec((LANES, lk), lambda b, h, i: (0, b)),
                  pl.BlockSpec((lk, heads * V_DIM), lambda b, h, i: (b, h))],
        out_specs=pl.BlockSpec((tq, heads * V_DIM), lambda b, h, i: (b * nq + i, h)),
        out_shape=jax.ShapeDtypeStruct((n_batch * lq, N_HEADS * V_DIM), BF16),
        compiler_params=_params("arbitrary", "arbitrary", "arbitrary"),
        name="attention",
    )(q, kn_t, kr_t, v)


def _split_rows_specs(tm, width, n_ctx_tiles):
    ctx = pl.BlockSpec((tm, width), lambda j, i: (jnp.minimum(i, n_ctx_tiles - 1), 0))
    lat = pl.BlockSpec((tm, width), lambda j, i: (jnp.maximum(i - n_ctx_tiles, 0), 0))
    return ctx, lat


def _omla_kernel(ac_ref, al_ref, w_ref, g_ref, o_ref, *, n_ctx_tiles):
    def run(a_ref):
        p = jnp.dot(a_ref[...], w_ref[...], preferred_element_type=F32)
        o_ref[...] = p * g_ref[...].astype(F32)

    i = pl.program_id(1)
    pl.when(i < n_ctx_tiles)(lambda: run(ac_ref))
    pl.when(i >= n_ctx_tiles)(lambda: run(al_ref))


def _omla(o_ctx, o_lat, w, gates):
    tm, tn = 512, 1024
    n_ctx_tiles = N_CTX // tm
    ctx, lat = _split_rows_specs(tm, D_MODEL, n_ctx_tiles)
    return pl.pallas_call(
        functools.partial(_omla_kernel, n_ctx_tiles=n_ctx_tiles),
        grid=(D_MODEL // tn, N_TOK // tm),
        in_specs=[ctx, lat,
                  pl.BlockSpec((D_MODEL, tn), lambda j, i: (0, j)),
                  pl.BlockSpec((tm, tn), lambda j, i: (i, j))],
        out_specs=pl.BlockSpec((tm, tn), lambda j, i: (i, j)),
        out_shape=jax.ShapeDtypeStruct((N_TOK, D_MODEL), F32),
        compiler_params=_params("arbitrary", "arbitrary"),
        name="omla",
    )(o_ctx, o_lat, w, gates)


def _softplus(x):
    return jnp.maximum(x, 0.0) + jnp.log1p(jnp.exp(-jnp.abs(x)))


def _conv_shift_mats():
    t = lax.broadcasted_iota(jnp.int32, (CHUNK, CHUNK + 2 * CONV_PAD), 0)
    j = lax.broadcasted_iota(jnp.int32, (CHUNK, CHUNK + 2 * CONV_PAD), 1)
    return [jnp.where(j == t + CONV_PAD + k - CONV_W // 2, 1.0, 0.0).astype(BF16) for k in range(CONV_W)]


def _conv_silu_chunk(src_ref, w_ref, b_ref, chunk, shift_mats):
    seq_len, width = src_ref.shape
    lo = chunk * CHUNK - CONV_PAD
    hi = (chunk + 1) * CHUNK + CONV_PAD
    parts = [src_ref[max(lo, 0):min(hi, seq_len), :]]
    if lo < 0:
        parts.insert(0, jnp.zeros((CONV_PAD, width), src_ref.dtype))
    if hi > seq_len:
        parts.append(jnp.zeros((CONV_PAD, width), src_ref.dtype))
    window = jnp.concatenate(parts, axis=0)
    mid = CONV_W // 2
    acc = b_ref[...] + w_ref[mid:mid + 1, :] * src_ref[chunk * CHUNK:(chunk + 1) * CHUNK, :].astype(F32)
    for k in range(CONV_W):
        if k != mid:
            acc = acc + w_ref[k:k + 1, :] * jnp.dot(shift_mats[k], window, preferred_element_type=F32)
    return _silu(acc)


def _head_pair_blockdiag(blk, lane):
    return jnp.concatenate([jnp.where(lane < SSD_HEADDIM, blk, 0.0),
                            jnp.where(lane >= SSD_HEADDIM, blk, 0.0)], axis=0)


def _ssd_kernel(*refs, seq_len, has_init):
    if has_init:
        (xs_ref, b_ref, c_ref, wx_ref, wb_ref, wc_ref, bx_ref, bb_ref, bc_ref,
         dtt_ref, bias_c_ref, a_c_ref, dsk_ref, init_ref,
         y_ref, fin_ref, rp_s, bt_s, cb_s, cbm_s, cst_s, gt_s, wt_s, csc_s, y_s) = refs
    else:
        (xs_ref, b_ref, c_ref, wx_ref, wb_ref, wc_ref, bx_ref, bb_ref, bc_ref,
         dtt_ref, bias_c_ref, a_c_ref, dsk_ref,
         y_ref, fin_ref, rp_s, bt_s, cb_s, cbm_s, cst_s, gt_s, wt_s, csc_s, y_s) = refs
        init_ref = None
    n_chunks = seq_len // CHUNK
    n_pairs = HEADS_PER_GROUP // 2
    lane = lax.broadcasted_iota(jnp.int32, (CHUNK, LANES), 1)

    shift_mats = _conv_shift_mats()
    for c in range(n_chunks):
        rows = slice(c * CHUNK, (c + 1) * CHUNK)
        x_c = _conv_silu_chunk(xs_ref, wx_ref, bx_ref, c, shift_mats)
        y_s[rows, :] = dsk_ref[...] * x_c
        for q in range(n_pairs):
            rp_s[c, q] = _head_pair_blockdiag(x_c[:, q * LANES:(q + 1) * LANES], lane).astype(BF16)
        b_c = _conv_silu_chunk(b_ref, wb_ref, bb_ref, c, shift_mats)
        bt_s[c] = b_c.T
        c_c = _conv_silu_chunk(c_ref, wc_ref, bc_ref, c, shift_mats)
        cb_s[c] = c_c.astype(BF16)
        cbm_s[c] = lax.dot_general(c_c.astype(BF16), b_c.astype(BF16), (((1,), (1,)), ((), ())),
                                   preferred_element_type=F32)

    a_col = a_c_ref[0]
    lane16 = lax.broadcasted_iota(jnp.int32, (2 * HEADS_PER_GROUP, CHUNK), 1)
    fwd_row = lax.broadcasted_iota(jnp.int32, (2 * HEADS_PER_GROUP, CHUNK), 0) < HEADS_PER_GROUP
    pad_rows = jnp.zeros((CHUNK - 2 * HEADS_PER_GROUP, CHUNK), F32)
    for c in range(n_chunks):
        dt_t = _softplus(dtt_ref[0, c] + bias_c_ref[0])
        pre = dt_t * a_col
        suf = pre
        shift = 1
        while shift < CHUNK:
            pre = pre + jnp.where(lane16 >= shift, pltpu.roll(pre, shift, axis=1), 0.0)
            suf = suf + jnp.where(lane16 < CHUNK - shift, pltpu.roll(suf, CHUNK - shift, axis=1), 0.0)
            shift *= 2
        cs_t = jnp.where(fwd_row, pre, suf)
        total = jnp.where(fwd_row, cs_t[:, CHUNK - 1:CHUNK], cs_t[:, 0:1])
        cst_s[c] = cs_t
        gt_s[c] = cs_t - jnp.log(dt_t)
        wt_s[c] = dt_t * jnp.exp(total - cs_t)
        csc_s[c * CHUNK:(c + 1) * CHUNK, :] = jnp.concatenate([cs_t, pad_rows], axis=0).T[
            :, :2 * HEADS_PER_GROUP]

    li = lax.broadcasted_iota(jnp.int32, (CHUNK, CHUNK), 0)
    si = lax.broadcasted_iota(jnp.int32, (CHUNK, CHUNK), 1)
    keep = (li >= si, li <= si)
    last = (CHUNK - 1, 0)

    def scan_chunk(d, c, state):
        rows = pl.ds(pl.multiple_of(c * CHUNK, CHUNK), CHUNK)
        cs = csc_s[rows, :]
        cs_t = cst_s[c]
        g_t = gt_s[c]
        w_t = wt_s[c]
        cb = cbm_s[c]
        b_t = bt_s[c]
        y_off = jnp.dot(cb_s[c], jnp.concatenate(state, axis=1).astype(BF16),
                        preferred_element_type=F32)
        new_state = []
        y_pairs = []
        for q in range(n_pairs):
            m_l, bw_l, e_l = [], [], []
            for r in (2 * q, 2 * q + 1):
                j = d * HEADS_PER_GROUP + r
                cs_l = jnp.broadcast_to(cs[:, j:j + 1], (CHUNK, CHUNK))
                m_l.append((cb * jnp.exp(jnp.where(keep[d], cs_l - g_t[j:j + 1, :], -jnp.inf))
                            ).astype(BF16))
                e_l.append(jnp.exp(cs_l))
                bw_l.append((b_t * w_t[j:j + 1, :]).astype(BF16))
            x_rhs = rp_s[c, q]
            y_diag = jnp.dot(jnp.concatenate(m_l, axis=1), x_rhs, preferred_element_type=F32)
            y_pairs.append(y_diag + y_off[:, q * LANES:(q + 1) * LANES]
                           * jnp.where(lane < SSD_HEADDIM, e_l[0], e_l[1]))
            new = jnp.dot(jnp.concatenate(bw_l, axis=1), x_rhs, preferred_element_type=F32)
            ja = d * HEADS_PER_GROUP + 2 * q
            etot = jnp.exp(cs_t[:, last[d]:last[d] + 1])
            e_row = jnp.where(lane[0:1, :] < SSD_HEADDIM, etot[ja:ja + 1, :], etot[ja + 1:ja + 2, :])
            new_state.append(state[q] * e_row + new)
        y_s[rows, :] += jnp.concatenate(y_pairs, axis=1)
        return tuple(new_state)

    def init_state(d):
        if has_init:
            s0 = init_ref[0, d].reshape(GROUP_W, D_STATE).T
        else:
            s0 = jnp.zeros((D_STATE, GROUP_W), F32)
        return tuple(s0[:, q * LANES:(q + 1) * LANES] for q in range(n_pairs))

    def step(i, carry):
        return (scan_chunk(0, i, carry[0]), scan_chunk(1, n_chunks - 1 - i, carry[1]))

    final = lax.fori_loop(0, n_chunks, step, (init_state(0), init_state(1)))
    for d in range(2):
        fin_ref[0, d] = jnp.concatenate(final[d], axis=1).T.reshape(
            HEADS_PER_GROUP, SSD_HEADDIM, D_STATE)
    y_ref[...] = y_s[...].astype(y_ref.dtype)


def _ssd(xbc, conv_w, conv_b, dt_cols, bias_c, a_c, dsk, init,
         *, n_batch, seq_len, row0):
    sb0 = row0 // seq_len
    n_chunks = seq_len // CHUNK
    n_dh = 2 * HEADS_PER_GROUP
    b_blk0 = D_INNER // D_STATE
    c_blk0 = (D_INNER + SSD_GROUPS * D_STATE) // D_STATE
    has_init = init is not None
    in_specs = [
        pl.BlockSpec((seq_len, GROUP_W), lambda b, g: (sb0 + b, g)),
        pl.BlockSpec((seq_len, D_STATE), lambda b, g: (sb0 + b, b_blk0 + g)),
        pl.BlockSpec((seq_len, D_STATE), lambda b, g: (sb0 + b, c_blk0 + g)),
        pl.BlockSpec((CONV_W, GROUP_W), lambda b, g: (0, g)),
        pl.BlockSpec((CONV_W, D_STATE), lambda b, g: (0, b_blk0 + g)),
        pl.BlockSpec((CONV_W, D_STATE), lambda b, g: (0, c_blk0 + g)),
        pl.BlockSpec((1, GROUP_W), lambda b, g: (0, g)),
        pl.BlockSpec((1, D_STATE), lambda b, g: (0, b_blk0 + g)),
        pl.BlockSpec((1, D_STATE), lambda b, g: (0, c_blk0 + g)),
        pl.BlockSpec((1, n_chunks, n_dh, CHUNK), lambda b, g: (g, sb0 + b, 0, 0)),
        pl.BlockSpec((1, n_dh, 1), lambda b, g: (g, 0, 0)),
        pl.BlockSpec((1, n_dh, 1), lambda b, g: (g, 0, 0)),
        pl.BlockSpec((1, GROUP_W), lambda b, g: (0, g)),
    ]
    args = [xbc, xbc, xbc, conv_w, conv_w, conv_w, conv_b, conv_b, conv_b,
            dt_cols, bias_c, a_c, dsk]
    state_spec = pl.BlockSpec((1, 2, HEADS_PER_GROUP, SSD_HEADDIM, D_STATE),
                              lambda b, g: (b, 0, g, 0, 0))
    if has_init:
        in_specs.append(state_spec)
        args.append(init)
    return pl.pallas_call(
        functools.partial(_ssd_kernel, seq_len=seq_len, has_init=has_init),
        grid=(n_batch, SSD_GROUPS),
        in_specs=in_specs,
        out_specs=[pl.BlockSpec((seq_len, GROUP_W), lambda b, g: (b, g)), state_spec],
        out_shape=[jax.ShapeDtypeStruct((n_batch * seq_len, D_INNER), BF16),
                   jax.ShapeDtypeStruct((n_batch, 2, SSD_HEADS, SSD_HEADDIM, D_STATE), F32)],
        scratch_shapes=[pltpu.VMEM((n_chunks, HEADS_PER_GROUP // 2, 2 * CHUNK, LANES), BF16),
                        pltpu.VMEM((n_chunks, D_STATE, CHUNK), F32),
                        pltpu.VMEM((n_chunks, CHUNK, D_STATE), BF16),
                        pltpu.VMEM((n_chunks, CHUNK, CHUNK), F32),
                        pltpu.VMEM((n_chunks, n_dh, CHUNK), F32),
                        pltpu.VMEM((n_chunks, n_dh, CHUNK), F32),
                        pltpu.VMEM((n_chunks, n_dh, CHUNK), F32),
                        pltpu.VMEM((seq_len, n_dh), F32),
                        pltpu.VMEM((seq_len, GROUP_W), F32)],
        compiler_params=_params("arbitrary", "arbitrary"),
        name="ssd",
    )(*args)


def _ossd_kernel(yc_ref, yl_ref, z_ref, ng_ref, w_ref, g_ref, om_ref, o_ref, a_s, *, n_ctx_tiles):
    i = pl.program_id(0)
    n_tiles = pl.num_programs(0) - 1

    def normalise(y_ref):
        slot = lax.rem(i, 2)
        for r0 in range(0, a_s.shape[1], CHUNK):
            rows = slice(r0, r0 + CHUNK)
            u = y_ref[rows, :].astype(F32) * z_ref[rows, :].astype(F32)
            a_s[slot, rows, :] = (_rms(u) * ng_ref[...]).astype(a_s.dtype)

    def project():
        p = jnp.dot(a_s[lax.rem(i + 1, 2)], w_ref[...], preferred_element_type=F32)
        o_ref[...] = (p * g_ref[...].astype(F32) + om_ref[...]).astype(o_ref.dtype)

    def both(y_ref):
        normalise(y_ref)
        project()

    pl.when(i == 0)(lambda: normalise(yc_ref))
    pl.when((i > 0) & (i < n_ctx_tiles))(lambda: both(yc_ref))
    pl.when((i >= n_ctx_tiles) & (i < n_tiles))(lambda: both(yl_ref))
    pl.when(i == n_tiles)(project)


def _ossd(y_ctx, y_lat, z, norm_g, w, gates, omla):
    tm = 256
    n_tiles = N_TOK // tm
    n_ctx_tiles = N_CTX // tm
    cur = lambda i: jnp.minimum(i, n_tiles - 1)
    prev = lambda i: jnp.maximum(i - 1, 0)
    return pl.pallas_call(
        functools.partial(_ossd_kernel, n_ctx_tiles=n_ctx_tiles),
        grid=(n_tiles + 1,),
        in_specs=[pl.BlockSpec((tm, D_INNER), lambda i: (jnp.minimum(i, n_ctx_tiles - 1), 0)),
                  pl.BlockSpec((tm, D_INNER),
                               lambda i: (jnp.clip(i - n_ctx_tiles, 0, n_tiles - n_ctx_tiles - 1), 0)),
                  pl.BlockSpec((tm, D_INNER), lambda i: (cur(i), 0)),
                  pl.BlockSpec((1, D_INNER), lambda i: (0, 0)),
                  pl.BlockSpec((D_INNER, D_MODEL), lambda i: (0, 0), pipeline_mode=pl.Buffered(1)),
                  pl.BlockSpec((tm, D_MODEL), lambda i: (prev(i), 1)),
                  pl.BlockSpec((tm, D_MODEL), lambda i: (prev(i), 0))],
        out_specs=pl.BlockSpec((tm, D_MODEL), lambda i: (prev(i), 0)),
        out_shape=jax.ShapeDtypeStruct((N_TOK, D_MODEL), BF16),
        scratch_shapes=[pltpu.VMEM((2, tm, D_INNER), BF16)],
        compiler_params=_params("arbitrary"),
        name="ossd_merge",
    )(y_ctx, y_lat, z, norm_g.reshape(1, D_INNER), w, gates, omla)


def _pack_halves(v):
    n = v.shape[1] // 2
    lo = lax.bitcast_convert_type(v[:, :n].astype(BF16).astype(F32), jnp.uint32)
    hi = lax.bitcast_convert_type(v[:, n:].astype(BF16).astype(F32), jnp.uint32)
    return (lo >> 16) | (hi & jnp.uint32(0xFFFF0000))


def _unpack_halves(w):
    lo = lax.bitcast_convert_type(w << 16, F32).astype(BF16)
    hi = lax.bitcast_convert_type(w & jnp.uint32(0xFFFF0000), F32).astype(BF16)
    return lo, hi


def _outproj_kernel(a_ref, w_ref, xc_ref, xl_ref, pg_ref, g1_ref, p2_ref, sc_ref, sh_ref, wr_ref,
                    x1_ref, h2_ref, lg_ref, *, n_ctx_tiles):
    def run(x_ref):
        out = jnp.dot(a_ref[...], w_ref[...], preferred_element_type=F32)
        x1 = x_ref[...] + g1_ref[0] * (_rms(out) * pg_ref[...])
        x1_ref[...] = x1
        h2 = (_rms(x1) * p2_ref[...]) * (1.0 + sc_ref[0]) + sh_ref[0]
        h2_ref[...] = _pack_halves(h2)
        lg_ref[...] = jnp.dot(h2.astype(BF16), wr_ref[...], preferred_element_type=F32)

    i = pl.program_id(0)
    pl.when(i < n_ctx_tiles)(lambda: run(xc_ref))
    pl.when(i >= n_ctx_tiles)(lambda: run(xl_ref))


def _outproj(merged, w_out, x_ctx, x_lat, post1_g, pre2_g, mod, w_router_pad):
    tm = 256
    r = _mod_row(tm)
    row = lambda i: (i, 0)
    one = lambda i: (0, 0)
    ctx, lat = _ctx_lat_specs(tm, D_MODEL)
    return pl.pallas_call(
        functools.partial(_outproj_kernel, n_ctx_tiles=N_CTX // tm),
        grid=(N_TOK // tm,),
        in_specs=[pl.BlockSpec((tm, D_MODEL), row),
                  pl.BlockSpec((D_MODEL, D_MODEL), one),
                  ctx, lat,
                  pl.BlockSpec((1, D_MODEL), one),
                  pl.BlockSpec((1, 1, D_MODEL), lambda i: (r(i), 0, 2)),
                  pl.BlockSpec((1, D_MODEL), one),
                  pl.BlockSpec((1, 1, D_MODEL), lambda i: (r(i), 0, 4)),
                  pl.BlockSpec((1, 1, D_MODEL), lambda i: (r(i), 0, 3)),
                  pl.BlockSpec((D_MODEL, LANES), one)],
        out_specs=[pl.BlockSpec((tm, D_MODEL), row),
                   pl.BlockSpec((tm, D_MODEL // 2), row),
                   pl.BlockSpec((tm, LANES), row)],
        out_shape=[jax.ShapeDtypeStruct((N_TOK, D_MODEL), F32),
                   jax.ShapeDtypeStruct((N_TOK, D_MODEL // 2), jnp.uint32),
                   jax.ShapeDtypeStruct((N_TOK, LANES), F32)],
        compiler_params=_params("arbitrary"),
        name="outproj",
    )(merged, w_out, x_ctx, x_lat, post1_g.reshape(1, D_MODEL), mod, pre2_g.reshape(1, D_MODEL),
      mod, mod, w_router_pad)


def _ffn_kernel(idx_ref, pos_ref, h2_hbm, wg_ref, wu_ref, wd_ref, gv_ref, ys_hbm,
                xbuf, acc_ref, gsem, ssem):
    e = pl.program_id(0)
    f = pl.program_id(1)
    n_e = pl.num_programs(0)
    n_f = pl.num_programs(1)
    n_rows = xbuf.shape[1]
    half = n_f // 2
    per_step = n_rows // half
    slot = lax.rem(e, 2)
    other = 1 - slot

    def gather_copy(expert, buf, i):
        return pltpu.make_async_copy(h2_hbm.at[pl.ds(idx_ref[expert * n_rows + i], 1), :],
                                     xbuf.at[buf, pl.ds(i, 1), :], gsem.at[buf])

    def scatter_copy(expert, buf, i):
        return pltpu.make_async_copy(xbuf.at[buf, pl.ds(i, 1), :],
                                     ys_hbm.at[pl.ds(pos_ref[expert * n_rows + i], 1), :], ssem.at[buf])

    def start_all(copy, expert, buf):
        def body(i, carry):
            copy(expert, buf, i).start()
            return carry
        lax.fori_loop(0, n_rows, body, 0, unroll=8)

    def wait_gather(buf):
        pltpu.make_async_copy(h2_hbm.at[pl.ds(0, n_rows), :], xbuf.at[buf], gsem.at[buf]).wait()

    def wait_scatter(buf):
        pltpu.make_async_copy(xbuf.at[buf], ys_hbm.at[pl.ds(0, n_rows), :], ssem.at[buf]).wait()

    @pl.when((e == 0) & (f == 0))
    def _():
        acc_ref[...] = jnp.zeros_like(acc_ref)
        start_all(gather_copy, 0, 0)

    @pl.when(f == 0)
    def _():
        wait_gather(slot)

    @pl.when((f == half) & (e >= 1))
    def _():
        wait_scatter(other)

    def swiglu_step(start_row_copy):
        if start_row_copy is not None:
            for i in range(per_step):
                start_row_copy(i)
        wg = wg_ref[0].astype(BF16)
        wu = wu_ref[0].astype(BF16)
        wd = wd_ref[0].astype(BF16)
        for r0 in range(0, n_rows, FFN_ROWS):
            rows = slice(r0, r0 + FFN_ROWS)
            x = jnp.concatenate(_unpack_halves(xbuf[slot, rows, :]), axis=1)
            g = jnp.dot(x, wg, preferred_element_type=F32)
            u = jnp.dot(x, wu, preferred_element_type=F32)
            hid = (_silu(g) * u).astype(BF16)
            part = jnp.dot(hid, wd, preferred_element_type=F32)
            acc_ref[rows, :] = jnp.where(f == 0, part, acc_ref[rows, :] + part)

    drain = (f < half) & (e >= 1)
    refill = (f >= half) & (e + 1 < n_e)
    pl.when(drain)(lambda: swiglu_step(
        lambda i: scatter_copy(e - 1, other, f * per_step + i).start()))
    pl.when(refill)(lambda: swiglu_step(
        lambda i: gather_copy(e + 1, other, (f - half) * per_step + i).start()))
    pl.when(jnp.logical_not(drain | refill))(lambda: swiglu_step(None))

    @pl.when(f == n_f - 1)
    def _():
        for r0 in range(0, n_rows, FFN_ROWS):
            rows = slice(r0, r0 + FFN_ROWS)
            xbuf[slot, rows, :] = _pack_halves(acc_ref[rows, :] * gv_ref[0, rows, :])

    @pl.when((e == n_e - 1) & (f == n_f - 1))
    def _():
        start_all(scatter_copy, e, slot)
        wait_scatter(slot)


def _ffn(idx, pos, h2p, w_gate, w_up, w_down, gval):
    n_rows = gval.shape[1]
    tf = 256
    n_f = D_EXPERT // tf
    assert N_EXPERTS >= 2 and n_f % 2 == 0 and n_rows % (n_f // 2) == 0
    return pl.pallas_call(
        _ffn_kernel,
        grid_spec=pltpu.PrefetchScalarGridSpec(
            num_scalar_prefetch=2,
            grid=(N_EXPERTS, n_f),
            in_specs=[pl.BlockSpec(memory_space=pl.ANY),
                      pl.BlockSpec((1, D_MODEL, tf), lambda e, f, i, p: (e, 0, f)),
                      pl.BlockSpec((1, D_MODEL, tf), lambda e, f, i, p: (e, 0, f)),
                      pl.BlockSpec((1, tf, D_MODEL), lambda e, f, i, p: (e, f, 0)),
                      pl.BlockSpec((1, n_rows, 1), lambda e, f, i, p: (e, 0, 0))],
            out_specs=pl.BlockSpec(memory_space=pl.ANY),
            scratch_shapes=[pltpu.VMEM((2, n_rows, D_MODEL // 2), jnp.uint32),
                            pltpu.VMEM((n_rows, D_MODEL), F32),
                            pltpu.SemaphoreType.DMA((2,)),
                            pltpu.SemaphoreType.DMA((2,))]),
        out_shape=jax.ShapeDtypeStruct((N_EXPERTS * n_rows, D_MODEL // 2), jnp.uint32),
        compiler_params=_params("arbitrary", "arbitrary"),
        name="expert_ffn",
    )(idx, pos, h2p, w_gate, w_up, w_down, gval)


COMBINE_TILE = 256


def _combine_kernel(wt_ref, wb_ref, wfirst_ref, wlast_ref, wvalid_ref,
                    ys_ref, s0_ref, s1_ref, x1_ref, pg_ref, g2_ref, oc_ref, ol_ref, acc_ref,
                    *, n_ctx_tiles):
    w = pl.program_id(0)
    half = acc_ref.shape[1] // 2

    @pl.when(wfirst_ref[w] == 1)
    def _():
        acc_ref[...] = jnp.zeros_like(acc_ref)

    @pl.when(wvalid_ref[w] == 1)
    def _():
        row = wb_ref[w] * COMBINE_TILE + lax.broadcasted_iota(
            jnp.int32, (COMBINE_TILE, COMBINE_TILE), 1)
        own = (row >= s0_ref[...]) & (row < s1_ref[...])
        sel = jnp.where(own, 1.0, 0.0).astype(BF16)
        lo, hi = _unpack_halves(ys_ref[...])
        acc_ref[:, :half] += jnp.dot(sel, lo, preferred_element_type=F32)
        acc_ref[:, half:] += jnp.dot(sel, hi, preferred_element_type=F32)

    def finish(o_ref):
        o_ref[...] = x1_ref[...] + g2_ref[0] * (_rms(acc_ref[...]) * pg_ref[...])

    done = wlast_ref[w] == 1
    pl.when(done & (wt_ref[w] < n_ctx_tiles))(lambda: finish(oc_ref))
    pl.when(done & (wt_ref[w] >= n_ctx_tiles))(lambda: finish(ol_ref))


def _combine(plan, ys, seg0, seg1, x1, post2_g, mod, n_ctx):
    n_tok = x1.shape[0]
    tile = COMBINE_TILE
    n_ctx_tiles = n_ctx // tile
    r = _mod_row(tile)
    n_work = plan[0].shape[0]
    tok = lambda w, wt, wb, wf, wl, wv: (wt[w], 0)
    return pl.pallas_call(
        functools.partial(_combine_kernel, n_ctx_tiles=n_ctx_tiles),
        grid_spec=pltpu.PrefetchScalarGridSpec(
            num_scalar_prefetch=5,
            grid=(n_work,),
            in_specs=[pl.BlockSpec((tile, D_MODEL // 2), lambda w, wt, wb, wf, wl, wv: (wb[w], 0)),
                      pl.BlockSpec((tile, 1), tok),
                      pl.BlockSpec((tile, 1), tok),
                      pl.BlockSpec((tile, D_MODEL), tok),
                      pl.BlockSpec((1, D_MODEL), lambda w, wt, wb, wf, wl, wv: (0, 0)),
                      pl.BlockSpec((1, 1, D_MODEL), lambda w, wt, wb, wf, wl, wv: (r(wt[w]), 0, 5))],
            out_specs=[pl.BlockSpec((tile, D_MODEL), lambda w, wt, wb, wf, wl, wv:
                                    (jnp.minimum(wt[w], n_ctx_tiles - 1), 0)),
                       pl.BlockSpec((tile, D_MODEL), lambda w, wt, wb, wf, wl, wv:
                                    (jnp.maximum(wt[w] - n_ctx_tiles, 0), 0))],
            scratch_shapes=[pltpu.VMEM((tile, D_MODEL), F32)]),
        out_shape=[jax.ShapeDtypeStruct((n_ctx, D_MODEL), F32),
                   jax.ShapeDtypeStruct((n_tok - n_ctx, D_MODEL), F32)],
        compiler_params=_params("arbitrary"),
        name="combine",
    )(*plan, ys, seg0, seg1, x1, post2_g.reshape(1, D_MODEL), mod)


def _moe_plan(idx, n_tok):
    n_e, n_r = idx.shape
    tile = COMBINE_TILE
    chosen = jnp.zeros((n_e, n_tok), jnp.int32).at[jnp.arange(n_e)[:, None], idx].set(1)
    seg = jnp.concatenate([jnp.zeros((1,), jnp.int32), jnp.cumsum(chosen.sum(0))])
    before = jnp.cumsum(chosen, axis=0) - chosen
    pos = seg[idx] + jnp.take_along_axis(before, idx, axis=1)

    n_tiles = n_tok // tile
    n_blocks = n_e * n_r // tile
    off = seg[::tile]
    first_blk = jnp.minimum(off[:-1] // tile, n_blocks - 1)
    last_blk = jnp.maximum((off[1:] - 1) // tile, first_blk)
    n_blk = last_blk - first_blk + 1
    start = jnp.cumsum(n_blk) - n_blk
    total = n_blk.sum()
    w = jnp.arange(n_tiles + n_blocks, dtype=jnp.int32)
    valid = w < total
    wt = jnp.clip(jnp.searchsorted(start, w, side="right") - 1, 0, n_tiles - 1)
    wt = jnp.where(valid, wt, n_tiles - 1).astype(jnp.int32)
    wb = jnp.where(valid, first_blk[wt] + (w - start[wt]), n_blocks - 1).astype(jnp.int32)
    first = (valid & (w == start[wt])).astype(jnp.int32)
    last = (valid & (w == start[wt] + n_blk[wt] - 1)).astype(jnp.int32)
    return pos.astype(jnp.int32), seg, (wt, wb, first, last, valid.astype(jnp.int32))


def _rope_tables():
    rows = DEC_SEQ // GRID_W
    row = jnp.repeat(jnp.arange(rows), GRID_W).astype(F32)
    col = jnp.tile(jnp.arange(GRID_W), rows).astype(F32)
    half = QK_ROPE // 2
    inv = ROPE_THETA ** (-jnp.arange(0, half, 2, dtype=F32) / half)
    ang_r = row[:, None] * inv[None, :]
    ang_c = col[:, None] * inv[None, :]
    ang = jnp.concatenate([ang_r, ang_r, ang_c, ang_c], axis=-1)
    cos, sin = jnp.cos(ang), jnp.sin(ang)
    ident = 256
    one = lambda n: jnp.ones((DEC_SEQ, n), F32)
    zero = lambda n: jnp.zeros((DEC_SEQ, n), F32)

    def table(parts_cos, parts_sin, width):
        c = jnp.concatenate([jnp.ones((ident, width), F32), jnp.concatenate(parts_cos, -1)], 0)
        s = jnp.concatenate([jnp.zeros((ident, width), F32), jnp.concatenate(parts_sin, -1)], 0)
        return c, s

    cos_k, sin_k = table([cos, one(LANES - QK_ROPE)], [sin, zero(LANES - QK_ROPE)], LANES)
    pad = Q_HEAD_PAD - QK_NOPE - QK_ROPE
    cos_q, sin_q = table([one(QK_NOPE), cos, one(pad)], [zero(QK_NOPE), sin, zero(pad)], Q_HEAD_PAD)
    q_scale = (QK_NOPE + QK_ROPE) ** -0.5 * np.log2(np.e)
    return cos_k, sin_k, cos_q * q_scale, sin_q * q_scale


def _route(logits, n_tok):
    cap = CAP_FACTOR * n_tok // N_EXPERTS
    aff = jax.nn.softmax(logits, axis=-1)
    return lax.top_k(aff.T, cap)


def kernel(x_prompt, x_sample, cache_ckv, cache_krope, state_ssm, c, c_ctx, w_mod, b_mod,
           pre1_g, post1_g, pre2_g, post2_g, w_in, q_norm_g, w_q_b, kv_norm_g, w_kv_b, w_o_mla,
           conv_w, conv_b, dt_bias, a_log, d_skip, ssd_norm_g, w_o_ssd, w_out, w_router,
           w_e_gate, w_e_up, w_e_down):
    l = 0
    x_ctx = x_prompt.reshape(N_CTX, D_MODEL)
    x_lat = x_sample.reshape(N_LAT, D_MODEL)

    n_mod = 1 + DEC_BATCH
    cvec = jnp.concatenate([c_ctx[None, :], c, jnp.zeros((16 - n_mod, D_MODEL), F32)], 0)
    mod = _adaln(cvec, w_mod[l], b_mod[l]).reshape(16, 1, 6 * D_MODEL)

    wi = w_in[l]
    o_z = Q_RANK + KV_RANK + QK_ROPE
    o_xbc = o_z + D_INNER
    o_dt = o_xbc + XBC_DIM
    o_gate = o_dt + 2 * SSD_HEADS
    w_small = jnp.concatenate([wi[:, :o_z], jnp.zeros((D_MODEL, LANES - QK_ROPE), F32),
                               wi[:, o_dt:o_gate]], axis=1).astype(BF16)
    w_z = wi[:, o_z:o_xbc].astype(BF16)
    w_xbc = wi[:, o_xbc:o_dt].astype(BF16)
    w_gates = wi[:, o_gate:].astype(BF16)
    wq = w_q_b[l].reshape(Q_RANK, N_HEADS, QK_NOPE + QK_ROPE)
    wq = jnp.concatenate([wq, jnp.zeros((Q_RANK, N_HEADS, Q_HEAD_PAD - QK_NOPE - QK_ROPE), F32)],
                         axis=-1).reshape(Q_RANK, N_HEADS * Q_HEAD_PAD).astype(BF16)
    cos_k, sin_k, cos_q, sin_q = _rope_tables()

    h1 = _prenorm(x_ctx, x_lat, pre1_g[l], mod, 1, 0)
    cqn, ckv, kr, dt_raw = _inproj_small(h1, w_small, q_norm_g[l], kv_norm_g[l], cos_k, sin_k)
    z = _mm(h1, w_z, tm=512, tn=1024, out_dtype=BF16, act="silu", name="inproj_z")
    xbc = _mm(h1, w_xbc, tm=512, tn=1024, out_dtype=BF16, name="inproj_xbc")
    gates = _mm(h1, w_gates, tm=512, tn=1024, out_dtype=BF16, act="sigmoid", name="inproj_gates")

    q = _qproj(cqn, wq, cos_q, sin_q)
    ckv_b = ckv.astype(BF16)
    kr_b = kr.astype(BF16)
    keys_ctx = ckv_b[:N_CTX]
    keys_lat = jnp.concatenate([cache_ckv[:, l].astype(BF16),
                                ckv_b[N_CTX:].reshape(DEC_BATCH, DEC_SEQ, KV_RANK)], axis=1)
    keys_lat = keys_lat.reshape(DEC_BATCH * (PAST_LEN + DEC_SEQ), KV_RANK)
    kr_cache = jnp.concatenate([cache_krope[:, l].astype(BF16),
                                jnp.zeros((DEC_BATCH, PAST_LEN, LANES - QK_ROPE), BF16)], axis=-1)
    kr_lat = jnp.concatenate([kr_cache, kr_b[N_CTX:].reshape(DEC_BATCH, DEC_SEQ, LANES)], axis=1)
    kr_lat = kr_lat.reshape(DEC_BATCH * (PAST_LEN + DEC_SEQ), LANES)
    wkv = w_kv_b[l].reshape(KV_RANK, N_HEADS, QK_NOPE + V_DIM)
    wk_t = wkv[:, :, :QK_NOPE].reshape(KV_RANK, N_HEADS * QK_NOPE).T.astype(BF16)
    wv = wkv[:, :, QK_NOPE:].reshape(KV_RANK, N_HEADS * V_DIM).astype(BF16)
    knt_ctx = _mm_nt(wk_t, keys_ctx, tn=512, out_dtype=BF16, name="knt_ctx")
    knt_lat = _mm_nt(wk_t, keys_lat, tn=512, out_dtype=BF16, name="knt_lat")
    v_ctx = _mm(keys_ctx, wv, tm=512, tn=2048, out_dtype=BF16, name="v_ctx")
    v_lat = _mm(keys_lat, wv, tm=512, tn=2048, out_dtype=BF16, name="v_lat")
    o_ctx = _attention(q, knt_ctx, kr_b[:N_CTX].T, v_ctx, n_batch=BATCH, lq=SEQ, lk=SEQ, tq=SEQ,
                       heads=N_HEADS, q_row0=0)
    o_lat = _attention(q, knt_lat, kr_lat.T, v_lat, n_batch=DEC_BATCH, lq=DEC_SEQ,
                       lk=PAST_LEN + DEC_SEQ, tq=512, heads=2, q_row0=N_CTX)
    omla = _omla(o_ctx, o_lat, w_o_mla[l].astype(BF16), gates)

    n_dh = 2 * HEADS_PER_GROUP
    dt4 = dt_raw.reshape(N_TOK, 2, SSD_GROUPS, HEADS_PER_GROUP)
    dt_rows = dt4.transpose(2, 0, 1, 3).reshape(SSD_GROUPS, N_TOK, n_dh)
    dt_cols = dt_rows.reshape(SSD_GROUPS, N_TOK // CHUNK, CHUNK, n_dh).transpose(0, 1, 3, 2)
    per_group = lambda v: v.reshape(2, SSD_GROUPS, HEADS_PER_GROUP).transpose(1, 0, 2).reshape(
        SSD_GROUPS, n_dh)
    bias_g = per_group(dt_bias[l])
    a_g = per_group(-jnp.exp(a_log[l]))
    dsk = jnp.repeat(d_skip[l], SSD_HEADDIM).reshape(1, D_INNER)
    ssd_args = (xbc, conv_w[l], conv_b[l].reshape(1, XBC_DIM), dt_cols, bias_g[:, :, None],
                a_g[:, :, None], dsk)
    y_ctx, ssm_ctx = _ssd(*ssd_args, None, n_batch=BATCH, seq_len=SEQ, row0=0)
    y_lat, _ = _ssd(*ssd_args, state_ssm[:, l], n_batch=DEC_BATCH, seq_len=DEC_SEQ, row0=N_CTX)
    merged = _ossd(y_ctx, y_lat, z, ssd_norm_g[l], w_o_ssd[l].astype(BF16), gates, omla)

    w_router_pad = jnp.concatenate([w_router[l], jnp.zeros((D_MODEL, LANES - N_EXPERTS), F32)],
                                   axis=1).astype(BF16)
    x1, h2p, logits = _outproj(merged, w_out[l].astype(BF16), x_ctx, x_lat, post1_g[l], pre2_g[l],
                               mod, w_router_pad)

    logits = logits[:, :N_EXPERTS]
    gv_c, idx_c = _route(logits[:N_CTX], N_CTX)
    gv_l, idx_l = _route(logits[N_CTX:], N_LAT)
    idx = jnp.concatenate([idx_c, idx_l + N_CTX], axis=1)
    gval = jnp.concatenate([gv_c, gv_l], axis=1)
    pos, seg, plan = _moe_plan(idx, N_TOK)
    ys = _ffn(idx.reshape(-1), pos.reshape(-1), h2p, w_e_gate[l], w_e_up[l], w_e_down[l],
              gval[..., None])
    out_ctx, out_lat = _combine(plan, ys, seg[:-1].reshape(N_TOK, 1), seg[1:].reshape(N_TOK, 1), x1,
                                post2_g[l], mod, N_CTX)

    y_p = out_ctx.reshape(BATCH, SEQ, D_MODEL)
    y_s = out_lat.reshape(DEC_BATCH, DEC_SEQ, D_MODEL)
    new_ckv = ckv[:N_CTX].reshape(BATCH, 1, SEQ, KV_RANK)
    new_krope = kr[:N_CTX, :QK_ROPE].reshape(BATCH, 1, SEQ, QK_ROPE)
    new_ssm = ssm_ctx[:, None]
    return (y_p, y_s, new_ckv, new_krope, new_ssm)
```

```python
import functools

import jax
import jax.numpy as jnp
import numpy as np
from jax import lax
from jax.experimental import pallas as pl
from jax.experimental.pallas import tpu as pltpu

F32 = jnp.float32
BF16 = jnp.bfloat16

D_MODEL = 2048
BATCH = 16
SEQ = 256
DEC_BATCH = 8
DEC_SEQ = 1024
PAST_LEN = 512
GRID_W = 64
N_HEADS = 16
Q_RANK = 512
KV_RANK = 512
QK_NOPE = 128
QK_ROPE = 64
V_DIM = 128
ROPE_THETA = 10000.0
D_INNER = 2 * D_MODEL
SSD_HEADDIM = 64
SSD_HEADS = D_INNER // SSD_HEADDIM
SSD_GROUPS = 8
HEADS_PER_GROUP = SSD_HEADS // SSD_GROUPS
D_STATE = 128
CONV_W = 5
CHUNK = 128
XBC_DIM = D_INNER + 2 * SSD_GROUPS * D_STATE
N_EXPERTS = 16
CAP_FACTOR = 2
D_EXPERT = 1536
EPS = 1e-6

N_CTX = BATCH * SEQ
N_LAT = DEC_BATCH * DEC_SEQ
N_TOK = N_CTX + N_LAT
Q_HEAD_PAD = 256
GROUP_W = HEADS_PER_GROUP * SSD_HEADDIM
LANES = 128
VMEM_LIMIT = 56 * 1024 * 1024
CONV_PAD = 16
FFN_ROWS = 512


def _params(*sem):
    return pltpu.CompilerParams(dimension_semantics=sem, vmem_limit_bytes=VMEM_LIMIT)


def _rms(x):
    return x * lax.rsqrt(jnp.mean(x * x, axis=-1, keepdims=True) + EPS)


def _silu(x):
    return x * jax.nn.sigmoid(x)


def _mod_row(tm):
    n_ctx_tiles = N_CTX // tm
    per_lat = DEC_SEQ // tm
    return lambda i: jnp.where(i < n_ctx_tiles, 0, 1 + (i - n_ctx_tiles) // per_lat)


def _rope_row(tm):
    n_ctx_tiles = N_CTX // tm
    per_lat = DEC_SEQ // tm
    return lambda i: jnp.where(i < n_ctx_tiles, 0, 1 + (i - n_ctx_tiles) % per_lat)


def _rot_half(x):
    w = x.shape[1]
    nxt = pltpu.roll(x, w - 16, axis=1)
    prv = pltpu.roll(x, 16, axis=1)
    lane = lax.broadcasted_iota(jnp.int32, x.shape, 1)
    return jnp.where((lane % 32) < 16, -nxt, prv)


def _adaln_kernel(c_ref, w_ref, b_ref, o_ref):
    a = _silu(c_ref[...]).astype(BF16)
    o_ref[...] = jnp.dot(a, w_ref[...].astype(BF16), preferred_element_type=F32) + b_ref[...]


def _adaln(cvec, w_mod, b_mod):
    rows = cvec.shape[0]
    n = w_mod.shape[1]
    tn = 1024
    return pl.pallas_call(
        _adaln_kernel,
        grid=(n // tn,),
        in_specs=[pl.BlockSpec((rows, D_MODEL), lambda j: (0, 0)),
                  pl.BlockSpec((D_MODEL, tn), lambda j: (0, j)),
                  pl.BlockSpec((1, tn), lambda j: (0, j))],
        out_specs=pl.BlockSpec((rows, tn), lambda j: (0, j)),
        out_shape=jax.ShapeDtypeStruct((rows, n), F32),
        compiler_params=_params("arbitrary"),
        name="adaln",
    )(cvec, w_mod, b_mod.reshape(1, n))


def _ctx_lat_specs(tm, width):
    n_ctx_tiles = N_CTX // tm
    ctx = pl.BlockSpec((tm, width), lambda i: (jnp.minimum(i, n_ctx_tiles - 1), 0))
    lat = pl.BlockSpec((tm, width), lambda i: (jnp.maximum(i - n_ctx_tiles, 0), 0))
    return ctx, lat


def _prenorm_kernel(xc_ref, xl_ref, g_ref, sc_ref, sh_ref, o_ref, *, n_ctx_tiles):
    def run(x_ref):
        y = _rms(x_ref[...]) * g_ref[...]
        o_ref[...] = (y * (1.0 + sc_ref[0]) + sh_ref[0]).astype(o_ref.dtype)

    i = pl.program_id(0)
    pl.when(i < n_ctx_tiles)(lambda: run(xc_ref))
    pl.when(i >= n_ctx_tiles)(lambda: run(xl_ref))


def _prenorm(x_ctx, x_lat, gain, mod, sc_blk, sh_blk):
    tm = 256
    r = _mod_row(tm)
    ctx, lat = _ctx_lat_specs(tm, D_MODEL)
    return pl.pallas_call(
        functools.partial(_prenorm_kernel, n_ctx_tiles=N_CTX // tm),
        grid=(N_TOK // tm,),
        in_specs=[ctx, lat,
                  pl.BlockSpec((1, D_MODEL), lambda i: (0, 0)),
                  pl.BlockSpec((1, 1, D_MODEL), lambda i: (r(i), 0, sc_blk)),
                  pl.BlockSpec((1, 1, D_MODEL), lambda i: (r(i), 0, sh_blk))],
        out_specs=pl.BlockSpec((tm, D_MODEL), lambda i: (i, 0)),
        out_shape=jax.ShapeDtypeStruct((N_TOK, D_MODEL), BF16),
        compiler_params=_params("arbitrary"),
        name="prenorm",
    )(x_ctx, x_lat, gain.reshape(1, D_MODEL), mod, mod)


def _inproj_small_kernel(h_ref, w_ref, qg_ref, kvg_ref, cos_ref, sin_ref,
                         cqn_ref, ckv_ref, kr_ref, dt_ref):
    p = jnp.dot(h_ref[...], w_ref[...], preferred_element_type=F32)
    cqn_ref[...] = (_rms(p[:, :Q_RANK]) * qg_ref[...]).astype(cqn_ref.dtype)
    ckv_ref[...] = _rms(p[:, Q_RANK:Q_RANK + KV_RANK]) * kvg_ref[...]
    kr = p[:, Q_RANK + KV_RANK:Q_RANK + KV_RANK + LANES]
    kr_ref[...] = kr * cos_ref[...] + _rot_half(kr) * sin_ref[...]
    dt_ref[...] = p[:, Q_RANK + KV_RANK + LANES:]


def _inproj_small(h, w, q_norm_g, kv_norm_g, cos_k, sin_k):
    tm = 256
    n = w.shape[1]
    rr = _rope_row(tm)
    row = lambda i: (i, 0)
    return pl.pallas_call(
        _inproj_small_kernel,
        grid=(N_TOK // tm,),
        in_specs=[pl.BlockSpec((tm, D_MODEL), row),
                  pl.BlockSpec((D_MODEL, n), lambda i: (0, 0)),
                  pl.BlockSpec((1, Q_RANK), lambda i: (0, 0)),
                  pl.BlockSpec((1, KV_RANK), lambda i: (0, 0)),
                  pl.BlockSpec((tm, LANES), lambda i: (rr(i), 0)),
                  pl.BlockSpec((tm, LANES), lambda i: (rr(i), 0))],
        out_specs=[pl.BlockSpec((tm, Q_RANK), row),
                   pl.BlockSpec((tm, KV_RANK), row),
                   pl.BlockSpec((tm, LANES), row),
                   pl.BlockSpec((tm, LANES), row)],
        out_shape=[jax.ShapeDtypeStruct((N_TOK, Q_RANK), BF16),
                   jax.ShapeDtypeStruct((N_TOK, KV_RANK), F32),
                   jax.ShapeDtypeStruct((N_TOK, LANES), F32),
                   jax.ShapeDtypeStruct((N_TOK, LANES), F32)],
        compiler_params=_params("arbitrary"),
        name="inproj_small",
    )(h, w, q_norm_g.reshape(1, Q_RANK), kv_norm_g.reshape(1, KV_RANK), cos_k, sin_k)


O_Z = Q_RANK + KV_RANK + QK_ROPE
O_XBC = O_Z + D_INNER
O_DT = O_XBC + XBC_DIM
O_GATE = O_DT + 2 * SSD_HEADS
IN_DIM = O_GATE + 2 * D_MODEL
SMALL_W = Q_RANK + KV_RANK + LANES + 2 * SSD_HEADS


def _win_split_kernel(w_ref, small_ref, z_ref, xbc_ref, gates_ref):
    pad = jnp.zeros((w_ref.shape[0], LANES - QK_ROPE), F32)
    small_ref[...] = jnp.concatenate([w_ref[:, :O_Z], pad, w_ref[:, O_DT:O_GATE]],
                                     axis=1).astype(small_ref.dtype)
    z_ref[...] = w_ref[:, O_Z:O_XBC].astype(z_ref.dtype)
    xbc_ref[...] = w_ref[:, O_XBC:O_DT].astype(xbc_ref.dtype)
    gates_ref[...] = w_ref[:, O_GATE:].astype(gates_ref.dtype)


def _win_split(w_in):
    tk = 128
    widths = (SMALL_W, D_INNER, XBC_DIM, 2 * D_MODEL)
    return pl.pallas_call(
        _win_split_kernel,
        grid=(D_MODEL // tk,),
        in_specs=[pl.BlockSpec((tk, IN_DIM), lambda i: (i, 0))],
        out_specs=[pl.BlockSpec((tk, w), lambda i: (i, 0)) for w in widths],
        out_shape=[jax.ShapeDtypeStruct((D_MODEL, w), BF16) for w in widths],
        compiler_params=_params("arbitrary"),
        name="win_split",
    )(w_in)


def _mm_kernel(a_ref, w_ref, o_ref, *, act):
    p = jnp.dot(a_ref[...], w_ref[...], preferred_element_type=F32)
    if act == "sigmoid":
        p = jax.nn.sigmoid(p)
    elif act == "silu":
        p = _silu(p)
    o_ref[...] = p.astype(o_ref.dtype)


def _mm(a, w, *, tm, tn, out_dtype, act=None, name="mm"):
    m, k = a.shape
    n = w.shape[1]
    return pl.pallas_call(
        functools.partial(_mm_kernel, act=act),
        grid=(n // tn, m // tm),
        in_specs=[pl.BlockSpec((tm, k), lambda j, i: (i, 0)),
                  pl.BlockSpec((k, tn), lambda j, i: (0, j))],
        out_specs=pl.BlockSpec((tm, tn), lambda j, i: (i, j)),
        out_shape=jax.ShapeDtypeStruct((m, n), out_dtype),
        compiler_params=_params("arbitrary", "arbitrary"),
        name=name,
    )(a, w)


def _mm_nt_kernel(a_ref, b_ref, o_ref):
    o_ref[...] = lax.dot_general(a_ref[...], b_ref[...], (((1,), (1,)), ((), ())),
                                 preferred_element_type=F32).astype(o_ref.dtype)


def _mm_nt(a, b, *, tn, out_dtype, name):
    m, k = a.shape
    n = b.shape[0]
    return pl.pallas_call(
        _mm_nt_kernel,
        grid=(n // tn,),
        in_specs=[pl.BlockSpec((m, k), lambda j: (0, 0)),
                  pl.BlockSpec((tn, k), lambda j: (j, 0))],
        out_specs=pl.BlockSpec((m, tn), lambda j: (0, j)),
        out_shape=jax.ShapeDtypeStruct((m, n), out_dtype),
        compiler_params=_params("arbitrary"),
        name=name,
    )(a, b)


def _qproj_kernel(a_ref, w_ref, cos_ref, sin_ref, o_ref):
    p = jnp.dot(a_ref[...], w_ref[...], preferred_element_type=F32)
    cos = cos_ref[...]
    sin = sin_ref[...]
    for h in range(N_HEADS):
        nope = slice(h * Q_HEAD_PAD, h * Q_HEAD_PAD + QK_NOPE)
        rope = slice(h * Q_HEAD_PAD + QK_NOPE, (h + 1) * Q_HEAD_PAD)
        o_ref[:, nope] = (p[:, nope] * cos[:, :QK_NOPE]).astype(o_ref.dtype)
        pr = p[:, rope]
        o_ref[:, rope] = (pr * cos[:, QK_NOPE:] + _rot_half(pr) * sin[:, QK_NOPE:]).astype(o_ref.dtype)


def _qproj(cqn, wq, cos_q, sin_q):
    tm = 256
    n = wq.shape[1]
    rr = _rope_row(tm)
    return pl.pallas_call(
        _qproj_kernel,
        grid=(N_TOK // tm,),
        in_specs=[pl.BlockSpec((tm, Q_RANK), lambda i: (i, 0)),
                  pl.BlockSpec((Q_RANK, n), lambda i: (0, 0)),
                  pl.BlockSpec((tm, Q_HEAD_PAD), lambda i: (rr(i), 0)),
                  pl.BlockSpec((tm, Q_HEAD_PAD), lambda i: (rr(i), 0))],
        out_specs=pl.BlockSpec((tm, n), lambda i: (i, 0)),
        out_shape=jax.ShapeDtypeStruct((N_TOK, n), BF16),
        compiler_params=_params("arbitrary"),
        name="qproj",
    )(cqn, wq, cos_q, sin_q)


ATTN_Q_ROWS = 256


def _attn_kernel(q_ref, knt_ref, krt_ref, v_ref, o_ref, *, heads):
    kr_t = krt_ref[...]
    for h in range(heads):
        k_t = jnp.concatenate([knt_ref[h * QK_NOPE:(h + 1) * QK_NOPE, :], kr_t], axis=0)
        v = v_ref[:, h * V_DIM:(h + 1) * V_DIM]
        for r0 in range(0, q_ref.shape[0], ATTN_Q_ROWS):
            rows = slice(r0, r0 + ATTN_Q_ROWS)
            q = q_ref[rows, h * Q_HEAD_PAD:(h + 1) * Q_HEAD_PAD]
            s = jnp.dot(q, k_t, preferred_element_type=F32)
            p = jnp.exp2(s - jnp.max(s, axis=-1, keepdims=True))
            l = jnp.sum(p, axis=-1, keepdims=True)
            o = jnp.dot(p.astype(BF16), v, preferred_element_type=F32)
            o_ref[rows, h * V_DIM:(h + 1) * V_DIM] = (o / l).astype(o_ref.dtype)


def _attention(q, kn_t, kr_t, v, *, n_batch, lq, lk, tq, heads, q_row0):
    nq = lq // tq
    qb0 = q_row0 // tq
    return pl.pallas_call(
        functools.partial(_attn_kernel, heads=heads),
        grid=(n_batch, N_HEADS // heads, nq),
        in_specs=[pl.BlockSpec((tq, heads * Q_HEAD_PAD), lambda b, h, i: (qb0 + b * nq + i, h)),
                  pl.BlockSpec((heads * QK_NOPE, lk), lambda b, h, i: (h, b)),
                  pl.BlockSpec((LANES, lk), lambda b, h, i: (0, b)),
                  pl.BlockSpec((lk, heads * V_DIM), lambda b, h, i: (b, h))],
        out_specs=pl.BlockSpec((tq, heads * V_DIM), lambda b, h, i: (b * nq + i, h)),
        out_shape=jax.ShapeDtypeStruct((n_batch * lq, N_HEADS * V_DIM), BF16),
        compiler_params=_params("arbitrary", "arbitrary", "arbitrary"),
        name="attention",
    )(q, kn_t, kr_t, v)


def _split_rows_specs(tm, width, n_ctx_tiles):
    ctx = pl.BlockSpec((tm, width), lambda j, i: (jnp.minimum(i, n_ctx_tiles - 1), 0))
    lat = pl.BlockSpec((tm, width), lambda j, i: (jnp.maximum(i - n_ctx_tiles, 0), 0))
    return ctx, lat


def _omla_kernel(ac_ref, al_ref, w_ref, g_ref, o_ref, *, n_ctx_tiles):
    def run(a_ref):
        p = jnp.dot(a_ref[...], w_ref[...], preferred_element_type=F32)
        o_ref[...] = p * g_ref[...].astype(F32)

    i = pl.program_id(1)
    pl.when(i < n_ctx_tiles)(lambda: run(ac_ref))
    pl.when(i >= n_ctx_tiles)(lambda: run(al_ref))


def _omla(o_ctx, o_lat, w, gates):
    tm, tn = 512, 1024
    n_ctx_tiles = N_CTX // tm
    ctx, lat = _split_rows_specs(tm, D_MODEL, n_ctx_tiles)
    return pl.pallas_call(
        functools.partial(_omla_kernel, n_ctx_tiles=n_ctx_tiles),
        grid=(D_MODEL // tn, N_TOK // tm),
        in_specs=[ctx, lat,
                  pl.BlockSpec((D_MODEL, tn), lambda j, i: (0, j)),
                  pl.BlockSpec((tm, tn), lambda j, i: (i, j))],
        out_specs=pl.BlockSpec((tm, tn), lambda j, i: (i, j)),
        out_shape=jax.ShapeDtypeStruct((N_TOK, D_MODEL), F32),
        compiler_params=_params("arbitrary", "arbitrary"),
        name="omla",
    )(o_ctx, o_lat, w, gates)


def _softplus(x):
    return jnp.maximum(x, 0.0) + jnp.log1p(jnp.exp(-jnp.abs(x)))


def _conv_shift_mats():
    t = lax.broadcasted_iota(jnp.int32, (CHUNK, CHUNK + 2 * CONV_PAD), 0)
    j = lax.broadcasted_iota(jnp.int32, (CHUNK, CHUNK + 2 * CONV_PAD), 1)
    return [jnp.where(j == t + CONV_PAD + k - CONV_W // 2, 1.0, 0.0).astype(BF16) for k in range(CONV_W)]


def _conv_silu_chunk(src_ref, w_ref, b_ref, chunk, shift_mats):
    seq_len, width = src_ref.shape
    lo = chunk * CHUNK - CONV_PAD
    hi = (chunk + 1) * CHUNK + CONV_PAD
    parts = [src_ref[max(lo, 0):min(hi, seq_len), :]]
    if lo < 0:
        parts.insert(0, jnp.zeros((CONV_PAD, width), src_ref.dtype))
    if hi > seq_len:
        parts.append(jnp.zeros((CONV_PAD, width), src_ref.dtype))
    window = jnp.concatenate(parts, axis=0)
    mid = CONV_W // 2
    acc = b_ref[...] + w_ref[mid:mid + 1, :] * src_ref[chunk * CHUNK:(chunk + 1) * CHUNK, :].astype(F32)
    for k in range(CONV_W):
        if k != mid:
            acc = acc + w_ref[k:k + 1, :] * jnp.dot(shift_mats[k], window, preferred_element_type=F32)
    return _silu(acc)


def _head_pair_blockdiag(blk, lane):
    return jnp.concatenate([jnp.where(lane < SSD_HEADDIM, blk, 0.0),
                            jnp.where(lane >= SSD_HEADDIM, blk, 0.0)], axis=0)


def _ssd_kernel(*refs, seq_len, has_init):
    if has_init:
        (xs_ref, b_ref, c_ref, wx_ref, wb_ref, wc_ref, bx_ref, bb_ref, bc_ref,
         dtt_ref, bias_c_ref, a_c_ref, dsk_ref, init_ref,
         y_ref, fin_ref, rp_s, bt_s, cb_s, cbm_s, cst_s, gt_s, wt_s, csc_s, y_s) = refs
    else:
        (xs_ref, b_ref, c_ref, wx_ref, wb_ref, wc_ref, bx_ref, bb_ref, bc_ref,
         dtt_ref, bias_c_ref, a_c_ref, dsk_ref,
         y_ref, fin_ref, rp_s, bt_s, cb_s, cbm_s, cst_s, gt_s, wt_s, csc_s, y_s) = refs
        init_ref = None
    n_chunks = seq_len // CHUNK
    n_pairs = HEADS_PER_GROUP // 2
    lane = lax.broadcasted_iota(jnp.int32, (CHUNK, LANES), 1)

    shift_mats = _conv_shift_mats()
    for c in range(n_chunks):
        rows = slice(c * CHUNK, (c + 1) * CHUNK)
        x_c = _conv_silu_chunk(xs_ref, wx_ref, bx_ref, c, shift_mats)
        y_s[rows, :] = dsk_ref[...] * x_c
        for q in range(n_pairs):
            rp_s[c, q] = _head_pair_blockdiag(x_c[:, q * LANES:(q + 1) * LANES], lane).astype(BF16)
        b_c = _conv_silu_chunk(b_ref, wb_ref, bb_ref, c, shift_mats)
        bt_s[c] = b_c.T
        c_c = _conv_silu_chunk(c_ref, wc_ref, bc_ref, c, shift_mats)
        cb_s[c] = c_c.astype(BF16)
        cbm_s[c] = lax.dot_general(c_c.astype(BF16), b_c.astype(BF16), (((1,), (1,)), ((), ())),
                                   preferred_element_type=F32)

    a_col = a_c_ref[0]
    lane16 = lax.broadcasted_iota(jnp.int32, (2 * HEADS_PER_GROUP, CHUNK), 1)
    fwd_row = lax.broadcasted_iota(jnp.int32, (2 * HEADS_PER_GROUP, CHUNK), 0) < HEADS_PER_GROUP
    pad_rows = jnp.zeros((CHUNK - 2 * HEADS_PER_GROUP, CHUNK), F32)
    for c in range(n_chunks):
        dt_t = _softplus(dtt_ref[0, c] + bias_c_ref[0])
        pre = dt_t * a_col
        suf = pre
        shift = 1
        while shift < CHUNK:
            pre = pre + jnp.where(lane16 >= shift, pltpu.roll(pre, shift, axis=1), 0.0)
            suf = suf + jnp.where(lane16 < CHUNK - shift, pltpu.roll(suf, CHUNK - shift, axis=1), 0.0)
            shift *= 2
        cs_t = jnp.where(fwd_row, pre, suf)
        total = jnp.where(fwd_row, cs_t[:, CHUNK - 1:CHUNK], cs_t[:, 0:1])
        cst_s[c] = cs_t
        gt_s[c] = cs_t - jnp.log(dt_t)
        wt_s[c] = dt_t * jnp.exp(total - cs_t)
        csc_s[c * CHUNK:(c + 1) * CHUNK, :] = jnp.concatenate([cs_t, pad_rows], axis=0).T[
            :, :2 * HEADS_PER_GROUP]

    li = lax.broadcasted_iota(jnp.int32, (CHUNK, CHUNK), 0)
    si = lax.broadcasted_iota(jnp.int32, (CHUNK, CHUNK), 1)
    keep = (li >= si, li <= si)
    last = (CHUNK - 1, 0)

    def scan_chunk(d, c, state):
        rows = pl.ds(pl.multiple_of(c * CHUNK, CHUNK), CHUNK)
        cs = csc_s[rows, :]
        cs_t = cst_s[c]
        g_t = gt_s[c]
        w_t = wt_s[c]
        cb = cbm_s[c]
        b_t = bt_s[c]
        y_off = jnp.dot(cb_s[c], jnp.concatenate(state, axis=1).astype(BF16),
                        preferred_element_type=F32)
        new_state = []
        y_pairs = []
        for q in range(n_pairs):
            m_l, bw_l, e_l = [], [], []
            for r in (2 * q, 2 * q + 1):
                j = d * HEADS_PER_GROUP + r
                cs_l = jnp.broadcast_to(cs[:, j:j + 1], (CHUNK, CHUNK))
                m_l.append((cb * jnp.exp(jnp.where(keep[d], cs_l - g_t[j:j + 1, :], -jnp.inf))
                            ).astype(BF16))
                e_l.append(jnp.exp(cs_l))
                bw_l.append((b_t * w_t[j:j + 1, :]).astype(BF16))
            x_rhs = rp_s[c, q]
            y_diag = jnp.dot(jnp.concatenate(m_l, axis=1), x_rhs, preferred_element_type=F32)
            y_pairs.append(y_diag + y_off[:, q * LANES:(q + 1) * LANES]
                           * jnp.where(lane < SSD_HEADDIM, e_l[0], e_l[1]))
            new = jnp.dot(jnp.concatenate(bw_l, axis=1), x_rhs, preferred_element_type=F32)
            ja = d * HEADS_PER_GROUP + 2 * q
            etot = jnp.exp(cs_t[:, last[d]:last[d] + 1])
            e_row = jnp.where(lane[0:1, :] < SSD_HEADDIM, etot[ja:ja + 1, :], etot[ja + 1:ja + 2, :])
            new_state.append(state[q] * e_row + new)
        y_s[rows, :] += jnp.concatenate(y_pairs, axis=1)
        return tuple(new_state)

    def init_state(d):
        if has_init:
            s0 = init_ref[0, d].reshape(GROUP_W, D_STATE).T
        else:
            s0 = jnp.zeros((D_STATE, GROUP_W), F32)
        return tuple(s0[:, q * LANES:(q + 1) * LANES] for q in range(n_pairs))

    def step(i, carry):
        return (scan_chunk(0, i, carry[0]), scan_chunk(1, n_chunks - 1 - i, carry[1]))

    final = lax.fori_loop(0, n_chunks, step, (init_state(0), init_state(1)))
    for d in range(2):
        fin_ref[0, d] = jnp.concatenate(final[d], axis=1).T.reshape(
            HEADS_PER_GROUP, SSD_HEADDIM, D_STATE)
    y_ref[...] = y_s[...].astype(y_ref.dtype)


def _ssd(xbc, conv_w, conv_b, dt_cols, bias_c, a_c, dsk, init,
         *, n_batch, seq_len, row0):
    sb0 = row0 // seq_len
    n_chunks = seq_len // CHUNK
    n_dh = 2 * HEADS_PER_GROUP
    b_blk0 = D_INNER // D_STATE
    c_blk0 = (D_INNER + SSD_GROUPS * D_STATE) // D_STATE
    has_init = init is not None
    in_specs = [
        pl.BlockSpec((seq_len, GROUP_W), lambda b, g: (sb0 + b, g)),
        pl.BlockSpec((seq_len, D_STATE), lambda b, g: (sb0 + b, b_blk0 + g)),
        pl.BlockSpec((seq_len, D_STATE), lambda b, g: (sb0 + b, c_blk0 + g)),
        pl.BlockSpec((CONV_W, GROUP_W), lambda b, g: (0, g)),
        pl.BlockSpec((CONV_W, D_STATE), lambda b, g: (0, b_blk0 + g)),
        pl.BlockSpec((CONV_W, D_STATE), lambda b, g: (0, c_blk0 + g)),
        pl.BlockSpec((1, GROUP_W), lambda b, g: (0, g)),
        pl.BlockSpec((1, D_STATE), lambda b, g: (0, b_blk0 + g)),
        pl.BlockSpec((1, D_STATE), lambda b, g: (0, c_blk0 + g)),
        pl.BlockSpec((1, n_chunks, n_dh, CHUNK), lambda b, g: (g, sb0 + b, 0, 0)),
        pl.BlockSpec((1, n_dh, 1), lambda b, g: (g, 0, 0)),
        pl.BlockSpec((1, n_dh, 1), lambda b, g: (g, 0, 0)),
        pl.BlockSpec((1, GROUP_W), lambda b, g: (0, g)),
    ]
    args = [xbc, xbc, xbc, conv_w, conv_w, conv_w, conv_b, conv_b, conv_b,
            dt_cols, bias_c, a_c, dsk]
    state_spec = pl.BlockSpec((1, 2, HEADS_PER_GROUP, SSD_HEADDIM, D_STATE),
                              lambda b, g: (b, 0, g, 0, 0))
    if has_init:
        in_specs.append(state_spec)
        args.append(init)
    return pl.pallas_call(
        functools.partial(_ssd_kernel, seq_len=seq_len, has_init=has_init),
        grid=(n_batch, SSD_GROUPS),
        in_specs=in_specs,
        out_specs=[pl.BlockSpec((seq_len, GROUP_W), lambda b, g: (b, g)), state_spec],
        out_shape=[jax.ShapeDtypeStruct((n_batch * seq_len, D_INNER), BF16),
                   jax.ShapeDtypeStruct((n_batch, 2, SSD_HEADS, SSD_HEADDIM, D_STATE), F32)],
        scratch_shapes=[pltpu.VMEM((n_chunks, HEADS_PER_GROUP // 2, 2 * CHUNK, LANES), BF16),
                        pltpu.VMEM((n_chunks, D_STATE, CHUNK), F32),
                        pltpu.VMEM((n_chunks, CHUNK, D_STATE), BF16),
                        pltpu.VMEM((n_chunks, CHUNK, CHUNK), F32),
                        pltpu.VMEM((n_chunks, n_dh, CHUNK), F32),
                        pltpu.VMEM((n_chunks, n_dh, CHUNK), F32),
                        pltpu.VMEM((n_chunks, n_dh, CHUNK), F32),
                        pltpu.VMEM((seq_len, n_dh), F32),
                        pltpu.VMEM((seq_len, GROUP_W), F32)],
        compiler_params=_params("arbitrary", "arbitrary"),
        name="ssd",
    )(*args)


def _ossd_kernel(yc_ref, yl_ref, z_ref, ng_ref, w_ref, g_ref, om_ref, o_ref, a_s, *, n_ctx_tiles):
    i = pl.program_id(0)
    n_tiles = pl.num_programs(0) - 1

    def normalise(y_ref):
        slot = lax.rem(i, 2)
        for r0 in range(0, a_s.shape[1], CHUNK):
            rows = slice(r0, r0 + CHUNK)
            u = y_ref[rows, :].astype(F32) * z_ref[rows, :].astype(F32)
            a_s[slot, rows, :] = (_rms(u) * ng_ref[...]).astype(a_s.dtype)

    def project():
        p = jnp.dot(a_s[lax.rem(i + 1, 2)], w_ref[...], preferred_element_type=F32)
        o_ref[...] = (p * g_ref[...].astype(F32) + om_ref[...]).astype(o_ref.dtype)

    def both(y_ref):
        normalise(y_ref)
        project()

    pl.when(i == 0)(lambda: normalise(yc_ref))
    pl.when((i > 0) & (i < n_ctx_tiles))(lambda: both(yc_ref))
    pl.when((i >= n_ctx_tiles) & (i < n_tiles))(lambda: both(yl_ref))
    pl.when(i == n_tiles)(project)


def _ossd(y_ctx, y_lat, z, norm_g, w, gates, omla):
    tm = 256
    n_tiles = N_TOK // tm
    n_ctx_tiles = N_CTX // tm
    cur = lambda i: jnp.minimum(i, n_tiles - 1)
    prev = lambda i: jnp.maximum(i - 1, 0)
    return pl.pallas_call(
        functools.partial(_ossd_kernel, n_ctx_tiles=n_ctx_tiles),
        grid=(n_tiles + 1,),
        in_specs=[pl.BlockSpec((tm, D_INNER), lambda i: (jnp.minimum(i, n_ctx_tiles - 1), 0)),
                  pl.BlockSpec((tm, D_INNER),
                               lambda i: (jnp.clip(i - n_ctx_tiles, 0, n_tiles - n_ctx_tiles - 1), 0)),
                  pl.BlockSpec((tm, D_INNER), lambda i: (cur(i), 0)),
                  pl.BlockSpec((1, D_INNER), lambda i: (0, 0)),
                  pl.BlockSpec((D_INNER, D_MODEL), lambda i: (0, 0), pipeline_mode=pl.Buffered(1)),
                  pl.BlockSpec((tm, D_MODEL), lambda i: (prev(i), 1)),
                  pl.BlockSpec((tm, D_MODEL), lambda i: (prev(i), 0))],
        out_specs=pl.BlockSpec((tm, D_MODEL), lambda i: (prev(i), 0)),
        out_shape=jax.ShapeDtypeStruct((N_TOK, D_MODEL), BF16),
        scratch_shapes=[pltpu.VMEM((2, tm, D_INNER), BF16)],
        compiler_params=_params("arbitrary"),
        name="ossd_merge",
    )(y_ctx, y_lat, z, norm_g.reshape(1, D_INNER), w, gates, omla)


def _pack_halves(v):
    n = v.shape[1] // 2
    lo = lax.bitcast_convert_type(v[:, :n].astype(BF16).astype(F32), jnp.uint32)
    hi = lax.bitcast_convert_type(v[:, n:].astype(BF16).astype(F32), jnp.uint32)
    return (lo >> 16) | (hi & jnp.uint32(0xFFFF0000))


def _unpack_halves(w):
    lo = lax.bitcast_convert_type(w << 16, F32).astype(BF16)
    hi = lax.bitcast_convert_type(w & jnp.uint32(0xFFFF0000), F32).astype(BF16)
    return lo, hi


def _outproj_kernel(a_ref, w_ref, xc_ref, xl_ref, pg_ref, g1_ref, p2_ref, sc_ref, sh_ref, wr_ref,
                    x1_ref, h2_ref, lg_ref, *, n_ctx_tiles):
    def run(x_ref):
        out = jnp.dot(a_ref[...], w_ref[...], preferred_element_type=F32)
        x1 = x_ref[...] + g1_ref[0] * (_rms(out) * pg_ref[...])
        x1_ref[...] = x1
        h2 = (_rms(x1) * p2_ref[...]) * (1.0 + sc_ref[0]) + sh_ref[0]
        h2_ref[...] = _pack_halves(h2)
        lg_ref[...] = jnp.dot(h2.astype(BF16), wr_ref[...], preferred_element_type=F32)

    i = pl.program_id(0)
    pl.when(i < n_ctx_tiles)(lambda: run(xc_ref))
    pl.when(i >= n_ctx_tiles)(lambda: run(xl_ref))


def _outproj(merged, w_out, x_ctx, x_lat, post1_g, pre2_g, mod, w_router_pad):
    tm = 256
    r = _mod_row(tm)
    row = lambda i: (i, 0)
    one = lambda i: (0, 0)
    ctx, lat = _ctx_lat_specs(tm, D_MODEL)
    return pl.pallas_call(
        functools.partial(_outproj_kernel, n_ctx_tiles=N_CTX // tm),
        grid=(N_TOK // tm,),
        in_specs=[pl.BlockSpec((tm, D_MODEL), row),
                  pl.BlockSpec((D_MODEL, D_MODEL), one),
                  ctx, lat,
                  pl.BlockSpec((1, D_MODEL), one),
                  pl.BlockSpec((1, 1, D_MODEL), lambda i: (r(i), 0, 2)),
                  pl.BlockSpec((1, D_MODEL), one),
                  pl.BlockSpec((1, 1, D_MODEL), lambda i: (r(i), 0, 4)),
                  pl.BlockSpec((1, 1, D_MODEL), lambda i: (r(i), 0, 3)),
                  pl.BlockSpec((D_MODEL, LANES), one)],
        out_specs=[pl.BlockSpec((tm, D_MODEL), row),
                   pl.BlockSpec((tm, D_MODEL // 2), row),
                   pl.BlockSpec((tm, LANES), row)],
        out_shape=[jax.ShapeDtypeStruct((N_TOK, D_MODEL), F32),
                   jax.ShapeDtypeStruct((N_TOK, D_MODEL // 2), jnp.uint32),
                   jax.ShapeDtypeStruct((N_TOK, LANES), F32)],
        compiler_params=_params("arbitrary"),
        name="outproj",
    )(merged, w_out, x_ctx, x_lat, post1_g.reshape(1, D_MODEL), mod, pre2_g.reshape(1, D_MODEL),
      mod, mod, w_router_pad)


def _ffn_kernel(idx_ref, pos_ref, h2_hbm, wg_ref, wu_ref, wd_ref, gv_ref, ys_hbm,
                xbuf, acc_ref, gsem, ssem):
    e = pl.program_id(0)
    f = pl.program_id(1)
    n_e = pl.num_programs(0)
    n_f = pl.num_programs(1)
    n_rows = xbuf.shape[1]
    half = n_f // 2
    per_step = n_rows // half
    slot = lax.rem(e, 2)
    other = 1 - slot

    def gather_copy(expert, buf, i):
        return pltpu.make_async_copy(h2_hbm.at[pl.ds(idx_ref[expert * n_rows + i], 1), :],
                                     xbuf.at[buf, pl.ds(i, 1), :], gsem.at[buf])

    def scatter_copy(expert, buf, i):
        return pltpu.make_async_copy(xbuf.at[buf, pl.ds(i, 1), :],
                                     ys_hbm.at[pl.ds(pos_ref[expert * n_rows + i], 1), :], ssem.at[buf])

    def start_all(copy, expert, buf):
        def body(i, carry):
            copy(expert, buf, i).start()
            return carry
        lax.fori_loop(0, n_rows, body, 0, unroll=8)

    def wait_gather(buf):
        pltpu.make_async_copy(h2_hbm.at[pl.ds(0, n_rows), :], xbuf.at[buf], gsem.at[buf]).wait()

    def wait_scatter(buf):
        pltpu.make_async_copy(xbuf.at[buf], ys_hbm.at[pl.ds(0, n_rows), :], ssem.at[buf]).wait()

    @pl.when((e == 0) & (f == 0))
    def _():
        acc_ref[...] = jnp.zeros_like(acc_ref)
        start_all(gather_copy, 0, 0)

    @pl.when(f == 0)
    def _():
        wait_gather(slot)

    @pl.when((f == half) & (e >= 1))
    def _():
        wait_scatter(other)

    def swiglu_step(start_row_copy):
        if start_row_copy is not None:
            for i in range(per_step):
                start_row_copy(i)
        wg = wg_ref[0].astype(BF16)
        wu = wu_ref[0].astype(BF16)
        wd = wd_ref[0].astype(BF16)
        for r0 in range(0, n_rows, FFN_ROWS):
            rows = slice(r0, r0 + FFN_ROWS)
            x = jnp.concatenate(_unpack_halves(xbuf[slot, rows, :]), axis=1)
            g = jnp.dot(x, wg, preferred_element_type=F32)
            u = jnp.dot(x, wu, preferred_element_type=F32)
            hid = (_silu(g) * u).astype(BF16)
            part = jnp.dot(hid, wd, preferred_element_type=F32)
            acc_ref[rows, :] = jnp.where(f == 0, part, acc_ref[rows, :] + part)

    drain = (f < half) & (e >= 1)
    refill = (f >= half) & (e + 1 < n_e)
    pl.when(drain)(lambda: swiglu_step(
        lambda i: scatter_copy(e - 1, other, f * per_step + i).start()))
    pl.when(refill)(lambda: swiglu_step(
        lambda i: gather_copy(e + 1, other, (f - half) * per_step + i).start()))
    pl.when(jnp.logical_not(drain | refill))(lambda: swiglu_step(None))

    @pl.when(f == n_f - 1)
    def _():
        for r0 in range(0, n_rows, FFN_ROWS):
            rows = slice(r0, r0 + FFN_ROWS)
            xbuf[slot, rows, :] = _pack_halves(acc_ref[rows, :] * gv_ref[0, rows, :])

    @pl.when((e == n_e - 1) & (f == n_f - 1))
    def _():
        start_all(scatter_copy, e, slot)
        wait_scatter(slot)


def _ffn(idx, pos, h2p, w_gate, w_up, w_down, gval):
    n_rows = gval.shape[1]
    tf = 256
    n_f = D_EXPERT // tf
    assert N_EXPERTS >= 2 and n_f % 2 == 0 and n_rows % (n_f // 2) == 0
    return pl.pallas_call(
        _ffn_kernel,
        grid_spec=pltpu.PrefetchScalarGridSpec(
            num_scalar_prefetch=2,
            grid=(N_EXPERTS, n_f),
            in_specs=[pl.BlockSpec(memory_space=pl.ANY),
                      pl.BlockSpec((1, D_MODEL, tf), lambda e, f, i, p: (e, 0, f)),
                      pl.BlockSpec((1, D_MODEL, tf), lambda e, f, i, p: (e, 0, f)),
                      pl.BlockSpec((1, tf, D_MODEL), lambda e, f, i, p: (e, f, 0)),
                      pl.BlockSpec((1, n_rows, 1), lambda e, f, i, p: (e, 0, 0))],
            out_specs=pl.BlockSpec(memory_space=pl.ANY),
            scratch_shapes=[pltpu.VMEM((2, n_rows, D_MODEL // 2), jnp.uint32),
                            pltpu.VMEM((n_rows, D_MODEL), F32),
                            pltpu.SemaphoreType.DMA((2,)),
                            pltpu.SemaphoreType.DMA((2,))]),
        out_shape=jax.ShapeDtypeStruct((N_EXPERTS * n_rows, D_MODEL // 2), jnp.uint32),
        compiler_params=_params("arbitrary", "arbitrary"),
        name="expert_ffn",
    )(idx, pos, h2p, w_gate, w_up, w_down, gval)


COMBINE_TILE = 256


def _combine_kernel(wt_ref, wb_ref, wfirst_ref, wlast_ref, wvalid_ref,
                    ys_ref, s0_ref, s1_ref, x1_ref, pg_ref, g2_ref, oc_ref, ol_ref, acc_ref,
                    *, n_ctx_tiles):
    w = pl.program_id(0)
    half = acc_ref.shape[1] // 2

    @pl.when(wfirst_ref[w] == 1)
    def _():
        acc_ref[...] = jnp.zeros_like(acc_ref)

    @pl.when(wvalid_ref[w] == 1)
    def _():
        row = wb_ref[w] * COMBINE_TILE + lax.broadcasted_iota(
            jnp.int32, (COMBINE_TILE, COMBINE_TILE), 1)
        own = (row >= s0_ref[...]) & (row < s1_ref[...])
        sel = jnp.where(own, 1.0, 0.0).astype(BF16)
        lo, hi = _unpack_halves(ys_ref[...])
        acc_ref[:, :half] += jnp.dot(sel, lo, preferred_element_type=F32)
        acc_ref[:, half:] += jnp.dot(sel, hi, preferred_element_type=F32)

    def finish(o_ref):
        o_ref[...] = x1_ref[...] + g2_ref[0] * (_rms(acc_ref[...]) * pg_ref[...])

    done = wlast_ref[w] == 1
    pl.when(done & (wt_ref[w] < n_ctx_tiles))(lambda: finish(oc_ref))
    pl.when(done & (wt_ref[w] >= n_ctx_tiles))(lambda: finish(ol_ref))


def _combine(plan, ys, seg0, seg1, x1, post2_g, mod, n_ctx):
    n_tok = x1.shape[0]
    tile = COMBINE_TILE
    n_ctx_tiles = n_ctx // tile
    r = _mod_row(tile)
    n_work = plan[0].shape[0]
    tok = lambda w, wt, wb, wf, wl, wv: (wt[w], 0)
    return pl.pallas_call(
        functools.partial(_combine_kernel, n_ctx_tiles=n_ctx_tiles),
        grid_spec=pltpu.PrefetchScalarGridSpec(
            num_scalar_prefetch=5,
            grid=(n_work,),
            in_specs=[pl.BlockSpec((tile, D_MODEL // 2), lambda w, wt, wb, wf, wl, wv: (wb[w], 0)),
                      pl.BlockSpec((tile, 1), tok),
                      pl.BlockSpec((tile, 1), tok),
                      pl.BlockSpec((tile, D_MODEL), tok),
                      pl.BlockSpec((1, D_MODEL), lambda w, wt, wb, wf, wl, wv: (0, 0)),
                      pl.BlockSpec((1, 1, D_MODEL), lambda w, wt, wb, wf, wl, wv: (r(wt[w]), 0, 5))],
            out_specs=[pl.BlockSpec((tile, D_MODEL), lambda w, wt, wb, wf, wl, wv:
                                    (jnp.minimum(wt[w], n_ctx_tiles - 1), 0)),
                       pl.BlockSpec((tile, D_MODEL), lambda w, wt, wb, wf, wl, wv:
                                    (jnp.maximum(wt[w] - n_ctx_tiles, 0), 0))],
            scratch_shapes=[pltpu.VMEM((tile, D_MODEL), F32)]),
        out_shape=[jax.ShapeDtypeStruct((n_ctx, D_MODEL), F32),
                   jax.ShapeDtypeStruct((n_tok - n_ctx, D_MODEL), F32)],
        compiler_params=_params("arbitrary"),
        name="combine",
    )(*plan, ys, seg0, seg1, x1, post2_g.reshape(1, D_MODEL), mod)


def _moe_plan(idx, chosen):
    n_e, n_r = idx.shape
    n_tok = chosen.shape[1]
    tile = COMBINE_TILE
    seg = jnp.concatenate([jnp.zeros((1,), jnp.int32), jnp.cumsum(chosen.sum(0))])
    before = jnp.cumsum(chosen, axis=0) - chosen
    pos = seg[idx] + jnp.take_along_axis(before, idx, axis=1)

    n_tiles = n_tok // tile
    n_blocks = n_e * n_r // tile
    off = seg[::tile]
    first_blk = jnp.minimum(off[:-1] // tile, n_blocks - 1)
    last_blk = jnp.maximum((off[1:] - 1) // tile, first_blk)
    n_blk = last_blk - first_blk + 1
    start = jnp.cumsum(n_blk) - n_blk
    total = n_blk.sum()
    w = jnp.arange(n_tiles + n_blocks, dtype=jnp.int32)
    valid = w < total
    wt = jnp.clip(jnp.searchsorted(start, w, side="right") - 1, 0, n_tiles - 1)
    wt = jnp.where(valid, wt, n_tiles - 1).astype(jnp.int32)
    wb = jnp.where(valid, first_blk[wt] + (w - start[wt]), n_blocks - 1).astype(jnp.int32)
    first = (valid & (w == start[wt])).astype(jnp.int32)
    last = (valid & (w == start[wt] + n_blk[wt] - 1)).astype(jnp.int32)
    return pos.astype(jnp.int32), seg, (wt, wb, first, last, valid.astype(jnp.int32))


def _rope_tables():
    rows = DEC_SEQ // GRID_W
    row = jnp.repeat(jnp.arange(rows), GRID_W).astype(F32)
    col = jnp.tile(jnp.arange(GRID_W), rows).astype(F32)
    half = QK_ROPE // 2
    inv = ROPE_THETA ** (-jnp.arange(0, half, 2, dtype=F32) / half)
    ang_r = row[:, None] * inv[None, :]
    ang_c = col[:, None] * inv[None, :]
    ang = jnp.concatenate([ang_r, ang_r, ang_c, ang_c], axis=-1)
    cos, sin = jnp.cos(ang), jnp.sin(ang)
    ident = 256
    one = lambda n: jnp.ones((DEC_SEQ, n), F32)
    zero = lambda n: jnp.zeros((DEC_SEQ, n), F32)

    def table(parts_cos, parts_sin, width):
        c = jnp.concatenate([jnp.ones((ident, width), F32), jnp.concatenate(parts_cos, -1)], 0)
        s = jnp.concatenate([jnp.zeros((ident, width), F32), jnp.concatenate(parts_sin, -1)], 0)
        return c, s

    cos_k, sin_k = table([cos, one(LANES - QK_ROPE)], [sin, zero(LANES - QK_ROPE)], LANES)
    pad = Q_HEAD_PAD - QK_NOPE - QK_ROPE
    cos_q, sin_q = table([one(QK_NOPE), cos, one(pad)], [zero(QK_NOPE), sin, zero(pad)], Q_HEAD_PAD)
    q_scale = (QK_NOPE + QK_ROPE) ** -0.5 * np.log2(np.e)
    return cos_k, sin_k, cos_q * q_scale, sin_q * q_scale


def _route(logits, n_tok):
    cap = CAP_FACTOR * n_tok // N_EXPERTS
    aff_t = jax.nn.softmax(logits, axis=-1).T
    gv, idx = lax.top_k(aff_t, cap)
    thr = gv[:, -1:]
    cut = jnp.max(jnp.where(gv == thr, idx, -1), axis=1, keepdims=True)
    tok = jnp.arange(n_tok, dtype=idx.dtype)[None, :]
    chosen = (aff_t > thr) | ((aff_t == thr) & (tok <= cut))
    return gv, idx, chosen.astype(jnp.int32)


def kernel(x_prompt, x_sample, cache_ckv, cache_krope, state_ssm, c, c_ctx, w_mod, b_mod,
           pre1_g, post1_g, pre2_g, post2_g, w_in, q_norm_g, w_q_b, kv_norm_g, w_kv_b, w_o_mla,
           conv_w, conv_b, dt_bias, a_log, d_skip, ssd_norm_g, w_o_ssd, w_out, w_router,
           w_e_gate, w_e_up, w_e_down):
    l = 0
    x_ctx = x_prompt.reshape(N_CTX, D_MODEL)
    x_lat = x_sample.reshape(N_LAT, D_MODEL)

    n_mod = 1 + DEC_BATCH
    cvec = jnp.concatenate([c_ctx[None, :], c, jnp.zeros((16 - n_mod, D_MODEL), F32)], 0)
    mod = _adaln(cvec, w_mod[l], b_mod[l]).reshape(16, 1, 6 * D_MODEL)

    w_small, w_z, w_xbc, w_gates = _win_split(w_in[l])
    wq = w_q_b[l].reshape(Q_RANK, N_HEADS, QK_NOPE + QK_ROPE)
    wq = jnp.concatenate([wq, jnp.zeros((Q_RANK, N_HEADS, Q_HEAD_PAD - QK_NOPE - QK_ROPE), F32)],
                         axis=-1).reshape(Q_RANK, N_HEADS * Q_HEAD_PAD).astype(BF16)
    cos_k, sin_k, cos_q, sin_q = _rope_tables()

    h1 = _prenorm(x_ctx, x_lat, pre1_g[l], mod, 1, 0)
    cqn, ckv, kr, dt_raw = _inproj_small(h1, w_small, q_norm_g[l], kv_norm_g[l], cos_k, sin_k)
    z = _mm(h1, w_z, tm=1024, tn=2048, out_dtype=BF16, act="silu", name="inproj_z")
    xbc = _mm(h1, w_xbc, tm=1024, tn=2048, out_dtype=BF16, name="inproj_xbc")
    gates = _mm(h1, w_gates, tm=1024, tn=2048, out_dtype=BF16, act="sigmoid", name="inproj_gates")

    q = _qproj(cqn, wq, cos_q, sin_q)
    ckv_b = ckv.astype(BF16)
    kr_b = kr.astype(BF16)
    keys_ctx = ckv_b[:N_CTX]
    keys_lat = jnp.concatenate([cache_ckv[:, l].astype(BF16),
                                ckv_b[N_CTX:].reshape(DEC_BATCH, DEC_SEQ, KV_RANK)], axis=1)
    keys_lat = keys_lat.reshape(DEC_BATCH * (PAST_LEN + DEC_SEQ), KV_RANK)
    kr_cache = jnp.concatenate([cache_krope[:, l].astype(BF16),
                                jnp.zeros((DEC_BATCH, PAST_LEN, LANES - QK_ROPE), BF16)], axis=-1)
    kr_lat = jnp.concatenate([kr_cache, kr_b[N_CTX:].reshape(DEC_BATCH, DEC_SEQ, LANES)], axis=1)
    kr_lat = kr_lat.reshape(DEC_BATCH * (PAST_LEN + DEC_SEQ), LANES)
    wkv = w_kv_b[l].reshape(KV_RANK, N_HEADS, QK_NOPE + V_DIM)
    wk_t = wkv[:, :, :QK_NOPE].reshape(KV_RANK, N_HEADS * QK_NOPE).T.astype(BF16)
    wv = wkv[:, :, QK_NOPE:].reshape(KV_RANK, N_HEADS * V_DIM).astype(BF16)
    knt_ctx = _mm_nt(wk_t, keys_ctx, tn=512, out_dtype=BF16, name="knt_ctx")
    knt_lat = _mm_nt(wk_t, keys_lat, tn=512, out_dtype=BF16, name="knt_lat")
    v_ctx = _mm(keys_ctx, wv, tm=512, tn=2048, out_dtype=BF16, name="v_ctx")
    v_lat = _mm(keys_lat, wv, tm=512, tn=2048, out_dtype=BF16, name="v_lat")
    o_ctx = _attention(q, knt_ctx, kr_b[:N_CTX].T, v_ctx, n_batch=BATCH, lq=SEQ, lk=SEQ, tq=SEQ,
                       heads=N_HEADS, q_row0=0)
    o_lat = _attention(q, knt_lat, kr_lat.T, v_lat, n_batch=DEC_BATCH, lq=DEC_SEQ,
                       lk=PAST_LEN + DEC_SEQ, tq=512, heads=2, q_row0=N_CTX)
    omla = _omla(o_ctx, o_lat, w_o_mla[l].astype(BF16), gates)

    n_dh = 2 * HEADS_PER_GROUP
    dt4 = dt_raw.reshape(N_TOK, 2, SSD_GROUPS, HEADS_PER_GROUP)
    dt_rows = dt4.transpose(2, 0, 1, 3).reshape(SSD_GROUPS, N_TOK, n_dh)
    dt_cols = dt_rows.reshape(SSD_GROUPS, N_TOK // CHUNK, CHUNK, n_dh).transpose(0, 1, 3, 2)
    per_group = lambda v: v.reshape(2, SSD_GROUPS, HEADS_PER_GROUP).transpose(1, 0, 2).reshape(
        SSD_GROUPS, n_dh)
    bias_g = per_group(dt_bias[l])
    a_g = per_group(-jnp.exp(a_log[l]))
    dsk = jnp.repeat(d_skip[l], SSD_HEADDIM).reshape(1, D_INNER)
    ssd_args = (xbc, conv_w[l], conv_b[l].reshape(1, XBC_DIM), dt_cols, bias_g[:, :, None],
                a_g[:, :, None], dsk)
    y_ctx, ssm_ctx = _ssd(*ssd_args, None, n_batch=BATCH, seq_len=SEQ, row0=0)
    y_lat, _ = _ssd(*ssd_args, state_ssm[:, l], n_batch=DEC_BATCH, seq_len=DEC_SEQ, row0=N_CTX)
    merged = _ossd(y_ctx, y_lat, z, ssd_norm_g[l], w_o_ssd[l].astype(BF16), gates, omla)

    w_router_pad = jnp.concatenate([w_router[l], jnp.zeros((D_MODEL, LANES - N_EXPERTS), F32)],
                                   axis=1).astype(BF16)
    x1, h2p, logits = _outproj(merged, w_out[l].astype(BF16), x_ctx, x_lat, post1_g[l], pre2_g[l],
                               mod, w_router_pad)

    logits = logits[:, :N_EXPERTS]
    gv_c, idx_c, chosen_c = _route(logits[:N_CTX], N_CTX)
    gv_l, idx_l, chosen_l = _route(logits[N_CTX:], N_LAT)
    idx = jnp.concatenate([idx_c, idx_l + N_CTX], axis=1)
    gval = jnp.concatenate([gv_c, gv_l], axis=1)
    pos, seg, plan = _moe_plan(idx, jnp.concatenate([chosen_c, chosen_l], axis=1))
    ys = _ffn(idx.reshape(-1), pos.reshape(-1), h2p, w_e_gate[l], w_e_up[l], w_e_down[l],
              gval[..., None])
    out_ctx, out_lat = _combine(plan, ys, seg[:-1].reshape(N_TOK, 1), seg[1:].reshape(N_TOK, 1), x1,
                                post2_g[l], mod, N_CTX)

    y_p = out_ctx.reshape(BATCH, SEQ, D_MODEL)
    y_s = out_lat.reshape(DEC_BATCH, DEC_SEQ, D_MODEL)
    new_ckv = ckv[:N_CTX].reshape(BATCH, 1, SEQ, KV_RANK)
    new_krope = kr[:N_CTX, :QK_ROPE].reshape(BATCH, 1, SEQ, QK_ROPE)
    new_ssm = ssm_ctx[:, None]
    return (y_p, y_s, new_ckv, new_krope, new_ssm)
```
